```python
import numpy as np
import jax
import jax.numpy as jnp
from jax import lax

D_MODEL = 2048
BATCH = 4
SEQ = 4096
DEPTH = 1
DEC_BATCH = 128
DEC_SEQ = 4
PAST_LEN = 16384
PAGE_SIZE = 128

D_CONV = 1024
CONV_W = 3
N_HEADS = 16
D_NOPE = 128
D_ROPE = 64
D_QK = D_NOPE + D_ROPE
D_V = 128
Q_RANK = 512
KV_RANK = 512
ROPE_BASE = 10000.0
SCALE = D_QK ** -0.5
N_GROUPS = 8
EXP_PER_GROUP = 8
N_EXPERTS = N_GROUPS * EXP_PER_GROUP
TOP_K = 2
D_EXPERT = 1024
MOE_BLOCK = 128
Q_BLOCK = 128
EPS = 1e-6
NEG_INF = -1e30
COL_SIZES = (D_CONV, D_CONV, D_CONV, Q_RANK, KV_RANK, D_ROPE, D_MODEL, D_MODEL)
D_IN = sum(COL_SIZES)

kernel_name = "hybrid_conv_mla_hmoe_adaln_step"


def rmsnorm(x, g):
    xf = x.astype(jnp.float32)
    y = xf * lax.rsqrt(jnp.mean(xf * xf, axis=-1, keepdims=True) + EPS)
    return (y * g.astype(jnp.float32)).astype(x.dtype)


def rope(x, pos):
    half = D_ROPE // 2
    inv = ROPE_BASE ** (-jnp.arange(half, dtype=jnp.float32) / half)
    ang = pos.astype(jnp.float32)[:, None] * inv[None, :]
    cos = jnp.cos(ang)[:, None, :]
    sin = jnp.sin(ang)[:, None, :]
    xf = x.astype(jnp.float32)
    x1, x2 = xf[..., :half], xf[..., half:]
    return jnp.concatenate([x1 * cos - x2 * sin, x2 * cos + x1 * sin], axis=-1).astype(x.dtype)


def qk_norm_rope(t, g, pos):
    t = rmsnorm(t, g)
    return jnp.concatenate([t[..., :D_NOPE], rope(t[..., D_NOPE:], pos)], axis=-1)


def mla_keys(ckv, kpe, pos, w_uk, g_k):
    k_nope = jnp.einsum('...lc,chd->...lhd', ckv, w_uk)
    k_pe = jnp.broadcast_to(kpe[..., None, :], k_nope.shape[:-1] + (D_ROPE,))
    return qk_norm_rope(jnp.concatenate([k_nope, k_pe], axis=-1), g_k, pos)


def adaln(c, w_ada, b_ada):
    mod = jax.nn.silu(c) @ w_ada + b_ada
    return jnp.split(mod[:, None, :], 6, axis=-1)


def causal_conv(u, hist, w):
    t = u.shape[1]
    ext = jnp.concatenate([hist.astype(u.dtype), u], axis=1)
    y = ext[:, CONV_W - 1:] * w[CONV_W - 1]
    for j in range(CONV_W - 1):
        y = y + ext[:, j:j + t] * w[j]
    return y, ext[:, t:]


def mixer_projections(h, pos, P):
    proj = h @ P['w_in']
    idx = np.cumsum(COL_SIZES)[:-1].tolist()
    hc, bg, cg, qd, kvd, kpe, ga, gb = jnp.split(proj, idx, axis=-1)
    q = (rmsnorm(qd, P['g_q_lat']) @ P['w_q_up']).reshape(h.shape[:-1] + (N_HEADS, D_QK))
    q = qk_norm_rope(q, P['g_qk_q'], pos)
    ckv = rmsnorm(kvd, P['g_kv_lat'])
    u = cg * hc
    return u, bg, q, ckv, kpe, ga, gb


def mixer_output(conv_out, bg, attn_o, ga, gb, P):
    y_conv = (bg * conv_out) @ P['w_conv_out']
    y_attn = attn_o.reshape(attn_o.shape[0], attn_o.shape[1], N_HEADS * D_V) @ P['w_attn_out']
    m = jax.nn.sigmoid(ga) * y_conv + jax.nn.sigmoid(gb) * y_attn
    return m @ P['w_o']


def prompt_attention(q, ckv, kpe, P):
    b, s = q.shape[0], q.shape[1]
    pos = jnp.arange(s)
    k = mla_keys(ckv, kpe, pos, P['w_uk'], P['g_qk_k'])
    v = jnp.einsum('bsc,chd->bshd', ckv, P['w_uv'])
    nb = s // Q_BLOCK
    qb = q.reshape(b, nb, Q_BLOCK, N_HEADS, D_QK).transpose(1, 0, 2, 3, 4)

    def block(args):
        qi, i = args
        qpos = i * Q_BLOCK + jnp.arange(Q_BLOCK)
        sc = jnp.einsum('bqhd,bkhd->bhqk', qi, k).astype(jnp.float32) * SCALE
        sc = jnp.where(pos[None, :] <= qpos[:, None], sc, NEG_INF)
        p = jax.nn.softmax(sc, axis=-1).astype(v.dtype)
        return jnp.einsum('bhqk,bkhd->bqhd', p, v)

    o = lax.map(block, (qb, jnp.arange(nb)))
    return o.transpose(1, 0, 2, 3, 4).reshape(b, s, N_HEADS, D_V)


def sample_attention(q, ckv_new, kpe_new, cache_ckv, cache_kpe, page_table, l, P):
    t = q.shape[1]
    page = cache_ckv.shape[2]
    past = page_table.shape[1] * page
    key_pos = jnp.arange(past + t)
    q_pos = past + jnp.arange(t)
    mask = key_pos[None, :] <= q_pos[:, None]

    def one(args):
        pages, qi, cn, kn = args
        ckv = jnp.concatenate([cache_ckv[l, pages].reshape(past, KV_RANK), cn.astype(cache_ckv.dtype)], axis=0)
        kpe = jnp.concatenate([cache_kpe[l, pages].reshape(past, D_ROPE), kn.astype(cache_kpe.dtype)], axis=0)
        k = mla_keys(ckv, kpe, key_pos, P['w_uk'], P['g_qk_k'])
        sc = jnp.einsum('qhd,khd->hqk', qi, k).astype(jnp.float32) * SCALE
        sc = jnp.where(mask[None], sc, NEG_INF)
        p = jax.nn.softmax(sc, axis=-1).astype(ckv.dtype)
        o_lat = jnp.einsum('hqk,kc->qhc', p, ckv)
        return jnp.einsum('qhc,chd->qhd', o_lat, P['w_uv'])

    return lax.map(one, (page_table, q, ckv_new, kpe_new))


def route(h, P):
    lg = (h @ P['w_route_group'] + P['b_route_group']).astype(jnp.float32)
    p_g = jax.nn.softmax(lg, axis=-1)
    g_star = jnp.argmax(lg, axis=-1)
    p_top = jnp.take_along_axis(p_g, g_star[:, None], axis=1)
    le = (h @ P['w_route_expert'] + P['b_route_expert']).astype(jnp.float32)
    le = le.reshape(-1, N_GROUPS, EXP_PER_GROUP)
    le_sel = jnp.take_along_axis(le, g_star[:, None, None], axis=1)[:, 0]
    vals, idx = lax.top_k(le_sel, TOP_K)
    w = p_top * jax.nn.softmax(vals, axis=-1)
    eid = g_star[:, None] * EXP_PER_GROUP + idx
    return eid.astype(jnp.int32), w.astype(h.dtype)


def moe_ffn(h, eid, ew, w_gate, w_up, w_down):
    n, d = h.shape
    nk = n * TOP_K
    flat_e = eid.reshape(nk)
    flat_t = jnp.repeat(jnp.arange(n, dtype=jnp.int32), TOP_K)
    flat_w = ew.reshape(nk)
    order = jnp.argsort(flat_e)
    se, st, sw = flat_e[order], flat_t[order], flat_w[order]
    counts = jnp.bincount(flat_e, length=N_EXPERTS)
    padded = (counts + MOE_BLOCK - 1) // MOE_BLOCK * MOE_BLOCK
    pad_end = jnp.cumsum(padded)
    pad_start = pad_end - padded
    start = jnp.cumsum(counts) - counts
    dest = pad_start[se] + jnp.arange(nk) - start[se]
    n_blocks = (nk + N_EXPERTS * (MOE_BLOCK - 1) + MOE_BLOCK - 1) // MOE_BLOCK
    n_slots = n_blocks * MOE_BLOCK
    slot_tok = jnp.full((n_slots,), n, jnp.int32).at[dest].set(st)
    slot_w = jnp.zeros((n_slots,), h.dtype).at[dest].set(sw)
    block_exp = jnp.minimum(jnp.searchsorted(pad_end, jnp.arange(n_blocks) * MOE_BLOCK, side='right'), N_EXPERTS - 1)
    h_pad = jnp.concatenate([h, jnp.zeros((1, d), h.dtype)], axis=0)
    xb = h_pad[slot_tok].reshape(n_blocks, MOE_BLOCK, d)

    def expert_block(args):
        xblk, e = args
        a = jax.nn.silu(xblk @ w_gate[e]) * (xblk @ w_up[e])
        return a @ w_down[e]

    yb = lax.map(expert_block, (xb, block_exp)).reshape(n_slots, d)
    out = jnp.zeros((n + 1, d), h.dtype).at[slot_tok].add(yb * slot_w[:, None])
    return out[:n]


def channel_mixer(x, shift, scale, gate, P):
    b, t, d = x.shape
    h = (rmsnorm(x, P['g_norm_ffn']) * (1 + scale) + shift).reshape(b * t, d)
    eid, ew = route(h, P)
    y = moe_ffn(h, eid, ew, P['w_gate_e'], P['w_up_e'], P['w_down_e'])
    return x + gate * y.reshape(b, t, d)


def prompt_layer(x, c, P):
    b, s, _ = x.shape
    sh1, sc1, gt1, sh2, sc2, gt2 = adaln(c, P['w_ada'], P['b_ada'])
    h = rmsnorm(x, P['g_norm_mix']) * (1 + sc1) + sh1
    pos = jnp.arange(s)
    u, bg, q, ckv, kpe, ga, gb = mixer_projections(h, pos, P)
    conv_out, conv_state = causal_conv(u, jnp.zeros((b, CONV_W - 1, D_CONV), u.dtype), P['w_conv'])
    attn = prompt_attention(q, ckv, kpe, P)
    x = x + gt1 * mixer_output(conv_out, bg, attn, ga, gb, P)
    x = channel_mixer(x, sh2, sc2, gt2, P)
    return x, conv_state, ckv, kpe


def sample_layer(x, c, conv_hist, cache_ckv, cache_kpe, page_table, l, P):
    t = x.shape[1]
    past = page_table.shape[1] * cache_ckv.shape[2]
    sh1, sc1, gt1, sh2, sc2, gt2 = adaln(c, P['w_ada'], P['b_ada'])
    h = rmsnorm(x, P['g_norm_mix']) * (1 + sc1) + sh1
    pos = past + jnp.arange(t)
    u, bg, q, ckv, kpe, ga, gb = mixer_projections(h, pos, P)
    conv_out, conv_state = causal_conv(u, conv_hist, P['w_conv'])
    attn = sample_attention(q, ckv, kpe, cache_ckv, cache_kpe, page_table, l, P)
    x = x + gt1 * mixer_output(conv_out, bg, attn, ga, gb, P)
    x = channel_mixer(x, sh2, sc2, gt2, P)
    return x, conv_state, ckv, kpe


def setup_inputs(seed: int = 0) -> dict:
    key = jax.random.key(seed)
    ks = jax.random.split(key, 40)
    f32 = jnp.float32
    n_pages = PAST_LEN // PAGE_SIZE
    n_pool = (DEC_BATCH * n_pages * 5) // 4

    def nrm(k, shape, scale):
        return jax.random.normal(k, shape, f32) * scale

    def gain(k, shape):
        return 1.0 + 0.05 * jax.random.normal(k, shape, f32)

    page_table = jax.random.permutation(ks[0], n_pool)[:DEC_BATCH * n_pages].reshape(DEC_BATCH, n_pages).astype(jnp.int32)
    return {
        'x_prompt': nrm(ks[1], (BATCH, SEQ, D_MODEL), 1.0),
        'x_sample': nrm(ks[2], (DEC_BATCH, DEC_SEQ, D_MODEL), 1.0),
        'state_conv': nrm(ks[3], (DEPTH, DEC_BATCH, CONV_W - 1, D_CONV), 1.0),
        'cache_ckv': nrm(ks[4], (DEPTH, n_pool, PAGE_SIZE, KV_RANK), 1.0),
        'cache_kpe': nrm(ks[5], (DEPTH, n_pool, PAGE_SIZE, D_ROPE), 1.0),
        'page_table': page_table,
        'c_prompt': nrm(ks[6], (BATCH, D_MODEL), 1.0),
        'c_sample': nrm(ks[7], (DEC_BATCH, D_MODEL), 1.0),
        'w_ada': nrm(ks[8], (DEPTH, D_MODEL, 6 * D_MODEL), 0.5 * D_MODEL ** -0.5),
        'b_ada': nrm(ks[9], (DEPTH, 6 * D_MODEL), 0.01),
        'g_norm_mix': gain(ks[10], (DEPTH, D_MODEL)),
        'g_norm_ffn': gain(ks[11], (DEPTH, D_MODEL)),
        'w_in': nrm(ks[12], (DEPTH, D_MODEL, D_IN), D_MODEL ** -0.5),
        'w_conv': nrm(ks[13], (DEPTH, CONV_W, D_CONV), CONV_W ** -0.5),
        'w_conv_out': nrm(ks[14], (DEPTH, D_CONV, D_MODEL), D_CONV ** -0.5),
        'g_q_lat': gain(ks[15], (DEPTH, Q_RANK)),
        'w_q_up': nrm(ks[16], (DEPTH, Q_RANK, N_HEADS * D_QK), Q_RANK ** -0.5),
        'g_kv_lat': gain(ks[17], (DEPTH, KV_RANK)),
        'w_uk': nrm(ks[18], (DEPTH, KV_RANK, N_HEADS, D_NOPE), KV_RANK ** -0.5),
        'w_uv': nrm(ks[19], (DEPTH, KV_RANK, N_HEADS, D_V), KV_RANK ** -0.5),
        'g_qk_q': gain(ks[20], (DEPTH, D_QK)),
        'g_qk_k': gain(ks[21], (DEPTH, D_QK)),
        'w_attn_out': nrm(ks[22], (DEPTH, N_HEADS * D_V, D_MODEL), (N_HEADS * D_V) ** -0.5),
        'w_o': nrm(ks[23], (DEPTH, D_MODEL, D_MODEL), D_MODEL ** -0.5),
        'w_route_group': nrm(ks[24], (DEPTH, D_MODEL, N_GROUPS), D_MODEL ** -0.5),
        'b_route_group': nrm(ks[25], (DEPTH, N_GROUPS), 0.01),
        'w_route_expert': nrm(ks[26], (DEPTH, D_MODEL, N_EXPERTS), D_MODEL ** -0.5),
        'b_route_expert': nrm(ks[27], (DEPTH, N_EXPERTS), 0.01),
        'w_gate_e': nrm(ks[28], (DEPTH, N_EXPERTS, D_MODEL, D_EXPERT), D_MODEL ** -0.5),
        'w_up_e': nrm(ks[29], (DEPTH, N_EXPERTS, D_MODEL, D_EXPERT), D_MODEL ** -0.5),
        'w_down_e': nrm(ks[30], (DEPTH, N_EXPERTS, D_EXPERT, D_MODEL), D_EXPERT ** -0.5),
    }


def reference(x_prompt, x_sample, state_conv, cache_ckv, cache_kpe, page_table, c_prompt, c_sample,
              w_ada, b_ada, g_norm_mix, g_norm_ffn, w_in, w_conv, w_conv_out, g_q_lat, w_q_up,
              g_kv_lat, w_uk, w_uv, g_qk_q, g_qk_k, w_attn_out, w_o, w_route_group, b_route_group,
              w_route_expert, b_route_expert, w_gate_e, w_up_e, w_down_e):
    xp, xs = x_prompt, x_sample
    conv_p, conv_s, ckv_p, kpe_p, ckv_s, kpe_s = [], [], [], [], [], []
    for l in range(DEPTH):
        P = dict(w_ada=w_ada[l], b_ada=b_ada[l], g_norm_mix=g_norm_mix[l], g_norm_ffn=g_norm_ffn[l],
                 w_in=w_in[l], w_conv=w_conv[l], w_conv_out=w_conv_out[l], g_q_lat=g_q_lat[l],
                 w_q_up=w_q_up[l], g_kv_lat=g_kv_lat[l], w_uk=w_uk[l], w_uv=w_uv[l],
                 g_qk_q=g_qk_q[l], g_qk_k=g_qk_k[l], w_attn_out=w_attn_out[l], w_o=w_o[l],
                 w_route_group=w_route_group[l], b_route_group=b_route_group[l],
                 w_route_expert=w_route_expert[l], b_route_expert=b_route_expert[l],
                 w_gate_e=w_gate_e[l], w_up_e=w_up_e[l], w_down_e=w_down_e[l])
        xp, cp, kvp, pep = prompt_layer(xp, c_prompt, P)
        xs, cs, kvs, pes = sample_layer(xs, c_sample, state_conv[l], cache_ckv, cache_kpe, page_table, l, P)
        conv_p.append(cp)
        conv_s.append(cs)
        ckv_p.append(kvp)
        kpe_p.append(pep)
        ckv_s.append(kvs)
        kpe_s.append(pes)
    new_conv_prompt = jnp.stack(conv_p)
    new_conv_sample = jnp.stack(conv_s)
    new_ckv_prompt = jnp.stack(ckv_p)
    new_kpe_prompt = jnp.stack(kpe_p)
    new_ckv_sample = jnp.stack(ckv_s)
    new_kpe_sample = jnp.stack(kpe_s)
    return (xp, xs, new_conv_prompt, new_conv_sample, new_ckv_prompt, new_kpe_prompt, new_ckv_sample, new_kpe_sample)
```

```python
import functools

import jax
import jax.numpy as jnp
import numpy as np
from jax import lax
from jax.experimental import pallas as pl
from jax.experimental.pallas import tpu as pltpu

EPS = 1e-6
ROPE_BASE = 10000.0
NEG_INF = -1e30
TOP_K = 2
BF16 = jnp.bfloat16
F32 = jnp.float32
VMEM_LIMIT_BYTES = 56 * 1024 * 1024
LANES = 128
SUBLANES = 8
ROW_TILE = 512
MOE_BLOCK = 128
PAGES_PER_STEP = 4


def _params(*sem):
    return pltpu.CompilerParams(dimension_semantics=sem, vmem_limit_bytes=VMEM_LIMIT_BYTES)


def _divisor_tile(n, pref, mult):
    best = None
    t = mult
    while t <= min(n, pref):
        if n % t == 0:
            best = t
        t += mult
    return best if best is not None else n


def _dot(a, b):
    return jnp.dot(a, b, preferred_element_type=F32)


def _dot_nt(a, b):
    return lax.dot_general(a, b, (((1,), (1,)), ((), ())), preferred_element_type=F32)


def _sigmoid(x):
    return 1.0 / (1.0 + jnp.exp(-x))


def _rms(x):
    return x * lax.rsqrt(jnp.mean(x * x, axis=-1, keepdims=True) + EPS)


def _expand_mod(m, rows):
    reps = rows // m.shape[0]
    return m if reps == 1 else jnp.concatenate([m] * reps, axis=0)


def _rope_rows(x, cos2, sin2):
    half = x.shape[1] // 2
    swapped = jnp.concatenate([x[:, half:], x[:, :half]], axis=1)
    return x * cos2 + swapped * sin2


def _ada_kernel(c_ref, w_ref, b_ref, o_ref):
    c = c_ref[...]
    s = (c * _sigmoid(c)).astype(BF16)
    o_ref[...] = _dot(s, w_ref[...].astype(BF16)) + b_ref[...]


def _adaln(c_rows, w_ada, b_ada):
    r, d = c_rows.shape
    n = w_ada.shape[1]
    tn = _divisor_tile(n, 1024, LANES)
    return pl.pallas_call(
        _ada_kernel,
        grid=(n // tn,),
        in_specs=[pl.BlockSpec((r, d), lambda j: (0, 0)),
                  pl.BlockSpec((d, tn), lambda j: (0, j)),
                  pl.BlockSpec((1, tn), lambda j: (0, j))],
        out_specs=pl.BlockSpec((r, tn), lambda j: (0, j)),
        out_shape=jax.ShapeDtypeStruct((r, n), F32),
        compiler_params=_params("parallel"),
    )(c_rows, w_ada, b_ada.reshape(1, n))


def _norm_kernel(x_ref, g_ref, sc_ref, sh_ref, o_ref):
    x = x_ref[...]
    rows = x.shape[0]
    y = _rms(x) * g_ref[...]
    y = y * (1.0 + _expand_mod(sc_ref[0], rows)) + _expand_mod(sh_ref[0], rows)
    o_ref[...] = y.astype(o_ref.dtype)


def _conv_kernel(h_ref, wh_ref, wb_ref, wc_ref, wconv_ref, hist_ref, z_ref, st_ref, carry_ref, *,
                 tiles_per_seq, n_prompt_tiles, nb):
    i = pl.program_id(0)
    j = pl.program_id(1)
    h = h_ref[...]
    u = _dot(h, wc_ref[...]) * _dot(h, wh_ref[...])
    bg = _dot(h, wb_ref[...])
    tm = u.shape[0]
    w = wconv_ref[...]

    @pl.when(i < n_prompt_tiles)
    def _():
        @pl.when(i % tiles_per_seq == 0)
        def _():
            carry_ref[j] = jnp.zeros(carry_ref.shape[1:], F32)

        tail = carry_ref[j]
        row8 = lax.broadcasted_iota(jnp.int32, (SUBLANES, 1), 0)

        def shifted(k):
            ur = pltpu.roll(u, k, 0)
            top = jnp.where(row8 < k, pltpu.roll(tail, k, 0), ur[:SUBLANES])
            return jnp.concatenate([top, ur[SUBLANES:]], axis=0)

        y = u * w[2:3] + shifted(2) * w[0:1] + shifted(1) * w[1:2]
        z_ref[...] = (bg * y).astype(z_ref.dtype)
        carry_ref[j] = u[tm - SUBLANES:]

    @pl.when(i >= n_prompt_tiles)
    def _():
        ext = jnp.concatenate([hist_ref[...], u], axis=0)
        y = ext[2 * nb:] * w[2:3] + ext[:tm] * w[0:1] + ext[nb:nb + tm] * w[1:2]
        z_ref[...] = (bg * y).astype(z_ref.dtype)

    st_ref[0] = u[tm - 2 * nb:]


def _q_kernel(h_ref, wqd_ref, gq_ref, wup_ref, gn_ref, gp_ref, cos_ref, sin_ref, qn_ref, qp_ref, *,
              n_heads, d_nope, d_rope, scale):
    qd = _dot(h_ref[...], wqd_ref[...])
    qdn = (_rms(qd) * gq_ref[...]).astype(BF16)
    qu = _dot(qdn, wup_ref[...])
    cos2 = cos_ref[...]
    sin2 = sin_ref[...]
    d_qk = d_nope + d_rope
    for hh in range(n_heads):
        qn = qu[:, hh * d_nope:(hh + 1) * d_nope]
        qp = qu[:, n_heads * d_nope + hh * d_rope:n_heads * d_nope + (hh + 1) * d_rope]
        ss = jnp.sum(qn * qn, axis=-1, keepdims=True) + jnp.sum(qp * qp, axis=-1, keepdims=True)
        r = lax.rsqrt(ss / d_qk + EPS)
        qn = qn * r * gn_ref[...]
        qp = _rope_rows(qp * r * gp_ref[...], cos2, sin2)
        qn_ref[:, hh * d_nope:(hh + 1) * d_nope] = (qn * scale).astype(qn_ref.dtype)
        qp_ref[hh] = (qp * scale).astype(qp_ref.dtype)


def _kv_kernel(h_ref, wkvd_ref, wkpe_ref, gkv_ref, wuk_ref, wuv_ref, gn_ref, gp_ref, cos_ref, sin_ref,
               ckv_ref, kpe_ref, kn_ref, kp_ref, v_ref, *, n_heads, d_nope, d_rope):
    h = h_ref[...]
    ckv = _rms(_dot(h, wkvd_ref[...])) * gkv_ref[...]
    kpe = _dot(h, wkpe_ref[...])
    ckv_ref[...] = ckv
    kpe_ref[...] = kpe
    cb = ckv.astype(BF16)
    kn_all = _dot(cb, wuk_ref[...])
    v_ref[...] = _dot(cb, wuv_ref[...]).astype(v_ref.dtype)
    kr = _rope_rows(kpe * gp_ref[...], cos_ref[...], sin_ref[...])
    sp = jnp.sum(kpe * kpe, axis=-1, keepdims=True)
    d_qk = d_nope + d_rope
    for hh in range(n_heads):
        kn = kn_all[:, hh * d_nope:(hh + 1) * d_nope]
        r = lax.rsqrt((jnp.sum(kn * kn, axis=-1, keepdims=True) + sp) / d_qk + EPS)
        kn_ref[:, hh * d_nope:(hh + 1) * d_nope] = (kn * r * gn_ref[...]).astype(kn_ref.dtype)
        kp_ref[hh] = (kr * r).astype(kp_ref.dtype)


def _flash_kernel(qn_ref, qp_ref, kn_ref, kp_ref, v_ref, o_ref, m_ref, l_ref, acc_ref, *, tq):
    qi = pl.program_id(2)
    q = jnp.concatenate([qn_ref[...], qp_ref[0]], axis=1)
    m_ref[...] = jnp.full(m_ref.shape, NEG_INF, F32)
    l_ref[...] = jnp.zeros(l_ref.shape, F32)
    acc_ref[...] = jnp.zeros(acc_ref.shape, F32)

    def chunk(c, diagonal):
        rows = pl.ds(pl.multiple_of(c * tq, tq), tq)
        k = jnp.concatenate([kn_ref[rows, :], kp_ref[0, rows, :]], axis=1)
        s = _dot_nt(q, k)
        if diagonal:
            row = lax.broadcasted_iota(jnp.int32, s.shape, 0)
            col = lax.broadcasted_iota(jnp.int32, s.shape, 1)
            s = jnp.where(col <= row, s, NEG_INF)
        m_prev = m_ref[...]
        m_new = jnp.maximum(m_prev, jnp.max(s, axis=-1, keepdims=True))
        a = jnp.exp(m_prev - m_new)
        p = jnp.exp(s - m_new)
        l_ref[...] = a * l_ref[...] + jnp.sum(p, axis=-1, keepdims=True)
        acc_ref[...] = a * acc_ref[...] + _dot(p.astype(BF16), v_ref[rows, :])
        m_ref[...] = m_new

    def body(c, carry):
        chunk(c, False)
        return carry

    lax.fori_loop(0, qi, body, 0)
    chunk(qi, True)
    o_ref[...] = (acc_ref[...] / l_ref[...]).astype(o_ref.dtype)


def _qabs_kernel(qn_ref, gn_ref, wuk_ref, o_ref):
    q = (qn_ref[...].astype(F32) * gn_ref[...]).astype(BF16)
    o_ref[0] = _dot_nt(q, wuk_ref[...]).astype(o_ref.dtype)


def _paged_kernel(pt_ref, qt_ref, qpe_ref, *refs, pps, n_heads, d_qk, t_new):
    del pt_ref
    ckv_refs = refs[:pps]
    kpe_refs = refs[pps:2 * pps]
    (cos_ref, sin_ref, cn_ref, kn_ref, cosn_ref, sinn_ref, maskn_ref, wukt_ref, gp_ref,
     o_ref, m_ref, l_ref, acc_ref) = refs[2 * pps:]
    p = pl.program_id(1)

    @pl.when(p == 0)
    def _():
        m_ref[...] = jnp.full(m_ref.shape, NEG_INF, F32)
        l_ref[...] = jnp.zeros(l_ref.shape, F32)
        acc_ref[...] = jnp.zeros(acc_ref.shape, F32)

    qt = qt_ref[0]
    qpe = qpe_ref[0]
    d_nope = wukt_ref.shape[0] // n_heads

    def attend(c, kp, cos2, sin2, mask):
        nk = c.shape[0]
        cb = c.astype(BF16)
        kt = _dot_nt(wukt_ref[...], cb)
        ssn = jnp.sum((kt * kt).reshape(n_heads, d_nope, nk), axis=1)
        sq = kp * kp
        hi = sq.astype(BF16)
        lo = (sq - hi.astype(F32)).astype(BF16)
        ones = jnp.ones((SUBLANES, 2 * kp.shape[1]), BF16)
        ssp = _dot_nt(ones, jnp.concatenate([hi, lo], axis=1))[0:1]
        rinv = lax.rsqrt((ssn + ssp) / d_qk + EPS)
        kg = kp * gp_ref[...]
        kc = jnp.concatenate([kg * cos2, kg * sin2], axis=1).astype(BF16)
        s = (_dot_nt(qt, cb) + _dot_nt(qpe, kc)) * jnp.concatenate([rinv] * t_new, axis=0)
        if mask is not None:
            s = jnp.where(mask > 0.0, s, NEG_INF)
        m_prev = m_ref[...]
        m_new = jnp.maximum(m_prev, jnp.max(s, axis=-1, keepdims=True))
        a = jnp.exp(m_prev - m_new)
        e = jnp.exp(s - m_new)
        l_ref[...] = a * l_ref[...] + jnp.sum(e, axis=-1, keepdims=True)
        acc_ref[...] = a * acc_ref[...] + _dot(e.astype(BF16), cb)
        m_ref[...] = m_new

    c_past = jnp.concatenate([r[0, 0] for r in ckv_refs], axis=0)
    kp_past = jnp.concatenate([r[0, 0] for r in kpe_refs], axis=0)
    attend(c_past, kp_past, cos_ref[...], sin_ref[...], None)

    @pl.when(p == pl.num_programs(1) - 1)
    def _():
        pad = maskn_ref.shape[1] - cn_ref.shape[1]
        c_new = jnp.concatenate([cn_ref[0], jnp.zeros((pad, cn_ref.shape[2]), F32)], axis=0)
        kp_new = jnp.concatenate([kn_ref[0], jnp.zeros((pad, kn_ref.shape[2]), F32)], axis=0)
        attend(c_new, kp_new, cosn_ref[...], sinn_ref[...], maskn_ref[...])
        o_ref[0] = (acc_ref[...] / l_ref[...]).astype(o_ref.dtype)


def _ouv_kernel(ol_ref, wuv_ref, o_ref):
    o_ref[...] = _dot(ol_ref[0], wuv_ref[...]).astype(o_ref.dtype)


def _mix_kernel(h_ref, z_ref, op_ref, os_ref, wga_ref, wgb_ref, wco_ref, wao_ref, m_ref, *, n_prompt_tiles):
    h = h_ref[...]
    ga = _dot(h, wga_ref[...])
    gb = _dot(h, wgb_ref[...])
    y_conv = _dot(z_ref[...], wco_ref[...])
    o = jnp.where(pl.program_id(0) < n_prompt_tiles, op_ref[...], os_ref[...])
    y_attn = _dot(o, wao_ref[...])
    m_ref[...] = (_sigmoid(ga) * y_conv + _sigmoid(gb) * y_attn).astype(m_ref.dtype)


def _wo_kernel(m_ref, wo_ref, x_ref, gt_ref, g_ref, sc_ref, sh_ref, wr_ref, br_ref, gid_ref,
               x1_ref, h2_ref, eid_ref, ew_ref, *, n_groups):
    rows = x_ref.shape[0]
    x1 = x_ref[...] + _expand_mod(gt_ref[0], rows) * _dot(m_ref[...], wo_ref[...])
    x1_ref[...] = x1
    h2 = _rms(x1) * g_ref[...]
    h2 = h2 * (1.0 + _expand_mod(sc_ref[0], rows)) + _expand_mod(sh_ref[0], rows)
    h2_ref[...] = h2
    logits = _dot(h2.astype(BF16), wr_ref[...]) + br_ref[...]
    icol = lax.broadcasted_iota(jnp.int32, logits.shape, 1)
    col = icol.astype(F32)
    gid = gid_ref[...]
    big = float(logits.shape[1])
    lg = jnp.where(col < n_groups, logits, NEG_INF)
    mg = jnp.max(lg, axis=-1, keepdims=True)
    g_star = jnp.min(jnp.where(lg == mg, col, big), axis=-1, keepdims=True)
    p_top = 1.0 / jnp.sum(jnp.exp(lg - mg), axis=-1, keepdims=True)
    le = jnp.where(gid == g_star, logits, NEG_INF)
    v1 = jnp.max(le, axis=-1, keepdims=True)
    i1 = jnp.min(jnp.where(le == v1, col, big), axis=-1, keepdims=True)
    le2 = jnp.where(col == i1, NEG_INF, le)
    v2 = jnp.max(le2, axis=-1, keepdims=True)
    i2 = jnp.min(jnp.where(le2 == v2, col, big), axis=-1, keepdims=True)
    e2 = jnp.exp(v2 - v1)
    w1 = p_top * (1.0 / (1.0 + e2))
    w2 = p_top * (e2 / (1.0 + e2))
    eid = jnp.where(icol == 0, i1 - n_groups, jnp.where(icol == 1, i2 - n_groups, 0.0))
    eid_ref[...] = eid.astype(jnp.int32)
    ew_ref[...] = jnp.where(icol == 0, w1, jnp.where(icol == 1, w2, 0.0))


def _row_gather(idx_ref, base, n, src_hbm, dst, sem):
    def body(r, carry):
        tok = idx_ref[base + r]
        pltpu.make_async_copy(src_hbm.at[pl.ds(tok, 1), :], dst.at[pl.ds(r, 1), :], sem).start()
        return carry
    lax.fori_loop(0, n, body, 0)


def _rows_wait(n, src_hbm, dst, sem):
    pltpu.make_async_copy(src_hbm.at[pl.ds(0, n), :], dst, sem).wait()


def _expert_changed(be_ref, b):
    return jnp.logical_or(b == 0, be_ref[b] != be_ref[jnp.maximum(b - 1, 0)])


def _moe_up_kernel(be_ref, tok_ref, h_hbm, wg_ref, wu_ref, a_ref, xbuf, wg_bf, wu_bf, sem):
    b = pl.program_id(0)
    nb = pl.num_programs(0)
    blk = xbuf.shape[1]
    slot = b % 2

    @pl.when(_expert_changed(be_ref, b))
    def _():
        wg_bf[...] = wg_ref[0].astype(BF16)
        wu_bf[...] = wu_ref[0].astype(BF16)

    @pl.when(b == 0)
    def _():
        _row_gather(tok_ref, 0, blk, h_hbm, xbuf.at[0], sem.at[0])

    @pl.when(b + 1 < nb)
    def _():
        _row_gather(tok_ref, (b + 1) * blk, blk, h_hbm, xbuf.at[1 - slot], sem.at[1 - slot])

    _rows_wait(blk, h_hbm, xbuf.at[slot], sem.at[slot])
    x = xbuf[slot].astype(BF16)
    g = _dot(x, wg_bf[...])
    u = _dot(x, wu_bf[...])
    a_ref[...] = (g * _sigmoid(g) * u).astype(a_ref.dtype)


def _moe_down_kernel(be_ref, a_ref, wd_ref, sw_ref, y_ref, wd_bf):
    @pl.when(_expert_changed(be_ref, pl.program_id(0)))
    def _():
        wd_bf[...] = wd_ref[0].astype(BF16)

    y_ref[...] = _dot(a_ref[...], wd_bf[...]) * sw_ref[...]


def _combine_kernel(d0_ref, d1_ref, y_hbm, x1_ref, gt_ref, o_ref, ybuf, sem):
    i = pl.program_id(0)
    n = pl.num_programs(0)
    rows = x1_ref.shape[0]
    slot = i % 2

    def start(step, s):
        _row_gather(d0_ref, step * rows, rows, y_hbm, ybuf.at[s, 0], sem.at[s])
        _row_gather(d1_ref, step * rows, rows, y_hbm, ybuf.at[s, 1], sem.at[s])

    @pl.when(i == 0)
    def _():
        start(0, 0)

    @pl.when(i + 1 < n)
    def _():
        start(i + 1, 1 - slot)

    _rows_wait(rows, y_hbm, ybuf.at[slot, 0], sem.at[slot])
    _rows_wait(rows, y_hbm, ybuf.at[slot, 1], sem.at[slot])
    y = ybuf[slot, 0] + ybuf[slot, 1]
    o_ref[...] = x1_ref[...] + _expand_mod(gt_ref[0], rows) * y


def _rope_tables(pos, d_rope):
    half = d_rope // 2
    inv = ROPE_BASE ** (-jnp.arange(half, dtype=F32) / half)
    ang = pos.astype(F32)[:, None] * inv[None, :]
    cos, sin = jnp.cos(ang), jnp.sin(ang)
    return jnp.concatenate([cos, cos], axis=1), jnp.concatenate([-sin, sin], axis=1)


def _layer(x_all, dims, mod_all, state_conv_l, cache_ckv, cache_kpe, page_table, l, P):
    (B, S, NB, T) = dims
    R, D = x_all.shape
    RP, RS = B * S, T * NB
    C = P['w_conv'].shape[1]
    QR = P['g_q_lat'].shape[0]
    KVR = P['g_kv_lat'].shape[0]
    H, DN = P['w_uk'].shape[1], P['w_uk'].shape[2]
    DV = P['w_uv'].shape[2]
    DR = cache_kpe.shape[-1]
    DQK = DN + DR
    scale = DQK ** -0.5
    PAGE = cache_ckv.shape[2]
    NPAGES = page_table.shape[1]
    PAST = NPAGES * PAGE
    G = P['w_route_group'].shape[1]
    E = P['w_route_expert'].shape[1]
    F = P['w_gate_e'].shape[2]
    MG = NB

    tm = _divisor_tile(int(np.gcd(S, RS)), ROW_TILE, MG)
    n_tiles, np_tiles = R // tm, RP // tm

    def mod_spec(chunk):
        def idx(i, *_):
            return (jnp.where(i < np_tiles, (i * tm) // S, B), 0, chunk)
        return pl.BlockSpec((1, MG, D), idx)

    def row_spec(width, tile=tm):
        return pl.BlockSpec((tile, width), lambda i, *_: (i, 0))

    def full_spec(shape):
        nd = len(shape)
        return pl.BlockSpec(shape, lambda *_: (0,) * nd)

    offs = np.cumsum([0, C, C, C, QR, KVR, DR, D, D])
    w_h, w_b, w_c, w_qd, w_kvd, w_kpe, w_ga, w_gb = [
        P['w_in'][:, offs[k]:offs[k + 1]].astype(BF16) for k in range(8)]
    w_up = P['w_q_up'].reshape(QR, H, DQK)
    w_up = jnp.concatenate([w_up[:, :, :DN].reshape(QR, H * DN), w_up[:, :, DN:].reshape(QR, H * DR)],
                           axis=1).astype(BF16)
    w_uk = P['w_uk'].reshape(KVR, H * DN).astype(BF16)
    w_uv = P['w_uv'].reshape(KVR, H * DV).astype(BF16)
    w_co = P['w_conv_out'].astype(BF16)
    w_ao = P['w_attn_out'].astype(BF16)
    w_o = P['w_o'].astype(BF16)
    gq_n, gq_p = P['g_qk_q'][:DN].reshape(1, DN), P['g_qk_q'][DN:].reshape(1, DR)
    gk_n, gk_p = P['g_qk_k'][:DN].reshape(1, DN), P['g_qk_k'][DN:].reshape(1, DR)

    pos_rows = jnp.concatenate([jnp.tile(jnp.arange(S), B), PAST + jnp.repeat(jnp.arange(T), NB)])
    cos_rows, sin_rows = _rope_tables(pos_rows, DR)

    h1 = pl.pallas_call(
        _norm_kernel,
        grid=(n_tiles,),
        in_specs=[row_spec(D), full_spec((1, D)), mod_spec(1), mod_spec(0)],
        out_specs=row_spec(D),
        out_shape=jax.ShapeDtypeStruct((R, D), BF16),
        compiler_params=_params("parallel"),
    )(x_all, P['g_norm_mix'].reshape(1, D), mod_all, mod_all)

    assert T >= 2 and S % RS == 0 and RS >= 2 * NB
    tc = _divisor_tile(C, 512, LANES)
    n_c = C // tc
    tiles_per_seq = S // RS
    n_ct = R // RS
    hist_t = state_conv_l.transpose(1, 0, 2).reshape(2 * NB, C)
    z_all, st_all = pl.pallas_call(
        functools.partial(_conv_kernel, tiles_per_seq=tiles_per_seq, n_prompt_tiles=RP // RS, nb=NB),
        grid=(n_ct, n_c),
        in_specs=[pl.BlockSpec((RS, D), lambda i, j: (i, 0)),
                  pl.BlockSpec((D, tc), lambda i, j: (0, j)),
                  pl.BlockSpec((D, tc), lambda i, j: (0, j)),
                  pl.BlockSpec((D, tc), lambda i, j: (0, j)),
                  pl.BlockSpec((3, tc), lambda i, j: (0, j)),
                  pl.BlockSpec((2 * NB, tc), lambda i, j: (0, j))],
        out_specs=[pl.BlockSpec((RS, tc), lambda i, j: (i, j)),
                   pl.BlockSpec((1, 2 * NB, tc), lambda i, j: (i, 0, j))],
        out_shape=[jax.ShapeDtypeStruct((R, C), BF16), jax.ShapeDtypeStruct((n_ct, 2 * NB, C), F32)],
        scratch_shapes=[pltpu.VMEM((n_c, SUBLANES, tc), F32)],
        compiler_params=_params("arbitrary", "arbitrary"),
    )(h1, w_h, w_b, w_c, P['w_conv'], hist_t)
    new_conv_p = st_all[tiles_per_seq - 1:RP // RS:tiles_per_seq, 2 * NB - 2:]
    new_conv_s = st_all[n_ct - 1].reshape(2, NB, C).transpose(1, 0, 2)
    sblk = RP // RS

    qn_all, qp_all = pl.pallas_call(
        functools.partial(_q_kernel, n_heads=H, d_nope=DN, d_rope=DR, scale=scale),
        grid=(n_tiles,),
        in_specs=[row_spec(D), full_spec((D, QR)), full_spec((1, QR)), full_spec((QR, H * DQK)),
                  full_spec((1, DN)), full_spec((1, DR)), row_spec(DR), row_spec(DR)],
        out_specs=[row_spec(H * DN), pl.BlockSpec((H, tm, DR), lambda i: (0, i, 0))],
        out_shape=[jax.ShapeDtypeStruct((R, H * DN), BF16), jax.ShapeDtypeStruct((H, R, DR), BF16)],
        compiler_params=_params("parallel"),
    )(h1, w_qd, P['g_q_lat'].reshape(1, QR), w_up, gq_n, gq_p, cos_rows, sin_rows)

    ckv_all, kpe_all, kn_all, kp_all, v_all = pl.pallas_call(
        functools.partial(_kv_kernel, n_heads=H, d_nope=DN, d_rope=DR),
        grid=(n_tiles,),
        in_specs=[row_spec(D), full_spec((D, KVR)), full_spec((D, DR)), full_spec((1, KVR)),
                  full_spec((KVR, H * DN)), full_spec((KVR, H * DV)), full_spec((1, DN)), full_spec((1, DR)),
                  row_spec(DR), row_spec(DR)],
        out_specs=[row_spec(KVR), row_spec(DR), row_spec(H * DN),
                   pl.BlockSpec((H, tm, DR), lambda i: (0, i, 0)), row_spec(H * DV)],
        out_shape=[jax.ShapeDtypeStruct((R, KVR), F32), jax.ShapeDtypeStruct((R, DR), F32),
                   jax.ShapeDtypeStruct((R, H * DN), BF16), jax.ShapeDtypeStruct((H, R, DR), BF16),
                   jax.ShapeDtypeStruct((R, H * DV), BF16)],
        compiler_params=_params("parallel"),
    )(h1, w_kvd, w_kpe, P['g_kv_lat'].reshape(1, KVR), w_uk, w_uv, gk_n, gk_p, cos_rows, sin_rows)

    tq = _divisor_tile(S, 512, SUBLANES)
    nq = S // tq
    o_p = pl.pallas_call(
        functools.partial(_flash_kernel, tq=tq),
        grid=(B, H, nq),
        in_specs=[pl.BlockSpec((tq, DN), lambda b, h, q: (b * nq + q, h)),
                  pl.BlockSpec((1, tq, DR), lambda b, h, q: (h, b * nq + q, 0)),
                  pl.BlockSpec((S, DN), lambda b, h, q: (b, h)),
                  pl.BlockSpec((1, S, DR), lambda b, h, q: (h, b, 0)),
                  pl.BlockSpec((S, DV), lambda b, h, q: (b, h))],
        out_specs=pl.BlockSpec((tq, DV), lambda b, h, q: (b * nq + q, h)),
        out_shape=jax.ShapeDtypeStruct((RP, H * DV), BF16),
        scratch_shapes=[pltpu.VMEM((tq, 1), F32), pltpu.VMEM((tq, 1), F32), pltpu.VMEM((tq, DV), F32)],
        compiler_params=_params("parallel", "parallel", "arbitrary"),
    )(qn_all, qp_all, kn_all, kp_all, v_all)

    qt = pl.pallas_call(
        _qabs_kernel,
        grid=(H,),
        in_specs=[pl.BlockSpec((RS, DN), lambda h: (sblk, h)), full_spec((1, DN)),
                  pl.BlockSpec((KVR, DN), lambda h: (0, h))],
        out_specs=pl.BlockSpec((1, RS, KVR), lambda h: (h, 0, 0)),
        out_shape=jax.ShapeDtypeStruct((H, RS, KVR), BF16),
        compiler_params=_params("parallel"),
    )(qn_all, gk_n, w_uk)
    QROWS = T * H
    qt = qt.reshape(H, T, NB, KVR).transpose(2, 1, 0, 3).reshape(NB, QROWS, KVR)
    qpe = qp_all[:, RP:].reshape(H, T, NB, DR).transpose(2, 1, 0, 3).reshape(NB, QROWS, DR)
    half = DR // 2
    qpe = jnp.concatenate([qpe, qpe[..., half:], -qpe[..., :half]], axis=-1)
    cos_past, sin_past = _rope_tables(jnp.arange(PAST), DR)
    cos_new, sin_new = _rope_tables(PAST + jnp.arange(PAGE), DR)
    sin_past = jnp.concatenate([sin_past[:, half:]] * 2, axis=1)
    sin_new = jnp.concatenate([sin_new[:, half:]] * 2, axis=1)
    t_pad = -(-T // SUBLANES) * SUBLANES
    ckv_s = ckv_all[RP:].reshape(T, NB, KVR).transpose(1, 0, 2)
    kpe_s = kpe_all[RP:].reshape(T, NB, DR).transpose(1, 0, 2)
    cn = jnp.pad(ckv_s, ((0, 0), (0, t_pad - T), (0, 0)))
    kn = jnp.pad(kpe_s, ((0, 0), (0, t_pad - T), (0, 0)))
    key_j = jnp.arange(PAGE)[None, :]
    row_t = (jnp.arange(QROWS) // H)[:, None]
    mask_new = ((key_j <= row_t) & (key_j < T)).astype(F32)
    w_ukt = w_uk.T
    pps = PAGES_PER_STEP if NPAGES % PAGES_PER_STEP == 0 else 1
    n_steps = NPAGES // pps

    def page_spec(k, width):
        return pl.BlockSpec((1, 1, PAGE, width), lambda b, p, pt: (l, pt[b, p * pps + k], 0, 0))

    def cfull(shape):
        nd = len(shape)
        return pl.BlockSpec(shape, lambda b, p, pt: (0,) * nd)

    o_lat = pl.pallas_call(
        functools.partial(_paged_kernel, pps=pps, n_heads=H, d_qk=DQK, t_new=T),
        grid_spec=pltpu.PrefetchScalarGridSpec(
            num_scalar_prefetch=1,
            grid=(NB, n_steps),
            in_specs=([pl.BlockSpec((1, QROWS, KVR), lambda b, p, pt: (b, 0, 0)),
                       pl.BlockSpec((1, QROWS, 2 * DR), lambda b, p, pt: (b, 0, 0))]
                      + [page_spec(k, KVR) for k in range(pps)]
                      + [page_spec(k, DR) for k in range(pps)]
                      + [pl.BlockSpec((pps * PAGE, DR), lambda b, p, pt: (p, 0)),
                         pl.BlockSpec((pps * PAGE, DR), lambda b, p, pt: (p, 0)),
                         pl.BlockSpec((1, t_pad, KVR), lambda b, p, pt: (b, 0, 0)),
                         pl.BlockSpec((1, t_pad, DR), lambda b, p, pt: (b, 0, 0)),
                         cfull((PAGE, DR)), cfull((PAGE, DR)), cfull((QROWS, PAGE)),
                         cfull((H * DN, KVR)), cfull((1, DR))]),
            out_specs=pl.BlockSpec((1, QROWS, KVR), lambda b, p, pt: (b, 0, 0)),
            scratch_shapes=[pltpu.VMEM((QROWS, 1), F32), pltpu.VMEM((QROWS, 1), F32),
                            pltpu.VMEM((QROWS, KVR), F32)]),
        out_shape=jax.ShapeDtypeStruct((NB, QROWS, KVR), BF16),
        compiler_params=_params("parallel", "arbitrary"),
    )(page_table, qt, qpe, *([cache_ckv] * pps), *([cache_kpe] * pps), cos_past, sin_past,
      cn, kn, cos_new, sin_new, mask_new, w_ukt, gk_p)
    o_lat = o_lat.reshape(NB, T, H, KVR).transpose(2, 1, 0, 3).reshape(H, RS, KVR)
    o_s = pl.pallas_call(
        _ouv_kernel,
        grid=(H,),
        in_specs=[pl.BlockSpec((1, RS, KVR), lambda h: (h, 0, 0)),
                  pl.BlockSpec((KVR, DV), lambda h: (0, h))],
        out_specs=pl.BlockSpec((RS, DV), lambda h: (0, h)),
        out_shape=jax.ShapeDtypeStruct((RS, H * DV), BF16),
        compiler_params=_params("parallel"),
    )(o_lat, w_uv)

    tn = _divisor_tile(D, 512, LANES)
    tmm = _divisor_tile(int(np.gcd(RP, RS)), 1024, SUBLANES)
    npm = RP // tmm
    m_all = pl.pallas_call(
        functools.partial(_mix_kernel, n_prompt_tiles=npm),
        grid=(R // tmm, D // tn),
        in_specs=[pl.BlockSpec((tmm, D), lambda i, j: (i, 0)),
                  pl.BlockSpec((tmm, C), lambda i, j: (i, 0)),
                  pl.BlockSpec((tmm, H * DV), lambda i, j: (jnp.minimum(i, npm - 1), 0)),
                  pl.BlockSpec((tmm, H * DV), lambda i, j: (jnp.maximum(i - npm, 0), 0)),
                  pl.BlockSpec((D, tn), lambda i, j: (0, j)),
                  pl.BlockSpec((D, tn), lambda i, j: (0, j)),
                  pl.BlockSpec((C, tn), lambda i, j: (0, j)),
                  pl.BlockSpec((H * DV, tn), lambda i, j: (0, j))],
        out_specs=pl.BlockSpec((tmm, tn), lambda i, j: (i, j)),
        out_shape=jax.ShapeDtypeStruct((R, D), BF16),
        compiler_params=_params("parallel", "parallel"),
    )(h1, z_all, o_p, o_s, w_ga, w_gb, w_co, w_ao)

    ncol = -(-(G + E) // LANES) * LANES
    w_r = jnp.pad(jnp.concatenate([P['w_route_group'], P['w_route_expert']], axis=1),
                  ((0, 0), (0, ncol - G - E))).astype(BF16)
    b_r = jnp.pad(jnp.concatenate([P['b_route_group'], P['b_route_expert']]), (0, ncol - G - E)).reshape(1, ncol)
    colv = np.arange(ncol)
    gid = np.where((colv >= G) & (colv < G + E), (colv - G) // (E // G), -1).astype(np.float32).reshape(1, ncol)
    x1_all, h2_all, eid_all, ew_all = pl.pallas_call(
        functools.partial(_wo_kernel, n_groups=G),
        grid=(n_tiles,),
        in_specs=[row_spec(D), full_spec((D, D)), row_spec(D), mod_spec(2), full_spec((1, D)),
                  mod_spec(4), mod_spec(3), full_spec((D, ncol)), full_spec((1, ncol)), full_spec((1, ncol))],
        out_specs=[row_spec(D), row_spec(D), row_spec(ncol), row_spec(ncol)],
        out_shape=[jax.ShapeDtypeStruct((R, D), F32), jax.ShapeDtypeStruct((R, D), F32),
                   jax.ShapeDtypeStruct((R, ncol), jnp.int32), jax.ShapeDtypeStruct((R, ncol), F32)],
        compiler_params=_params("parallel"),
    )(m_all, w_o, x_all, mod_all, P['g_norm_ffn'].reshape(1, D), mod_all, mod_all, w_r, b_r, jnp.asarray(gid))

    nk = R * TOP_K
    flat_e = eid_all[:, :TOP_K].reshape(nk)
    flat_w = ew_all[:, :TOP_K].reshape(nk)
    flat_t = jnp.repeat(jnp.arange(R, dtype=jnp.int32), TOP_K)
    order = jnp.argsort(flat_e)
    se, st, sw = flat_e[order], flat_t[order], flat_w[order]
    counts = jnp.bincount(flat_e, length=E)
    padded = (counts + MOE_BLOCK - 1) // MOE_BLOCK * MOE_BLOCK
    pad_end = jnp.cumsum(padded)
    pad_start = pad_end - padded
    start = jnp.cumsum(counts) - counts
    dest_sorted = (pad_start[se] + jnp.arange(nk) - start[se]).astype(jnp.int32)
    n_blocks = (nk + E * (MOE_BLOCK - 1) + MOE_BLOCK - 1) // MOE_BLOCK
    n_slots = n_blocks * MOE_BLOCK
    slot_tok = jnp.zeros((n_slots,), jnp.int32).at[dest_sorted].set(st)
    slot_w = jnp.zeros((n_slots,), F32).at[dest_sorted].set(sw)
    dest = jnp.zeros((nk,), jnp.int32).at[order].set(dest_sorted).reshape(R, TOP_K)
    block_exp = jnp.minimum(jnp.searchsorted(pad_end, jnp.arange(n_blocks) * MOE_BLOCK, side='right'),
                            E - 1).astype(jnp.int32)

    a_all = pl.pallas_call(
        _moe_up_kernel,
        grid_spec=pltpu.PrefetchScalarGridSpec(
            num_scalar_prefetch=2,
            grid=(n_blocks,),
            in_specs=[pl.BlockSpec(memory_space=pl.ANY),
                      pl.BlockSpec((1, D, F), lambda b, be, tok: (be[b], 0, 0)),
                      pl.BlockSpec((1, D, F), lambda b, be, tok: (be[b], 0, 0))],
            out_specs=pl.BlockSpec((MOE_BLOCK, F), lambda b, be, tok: (b, 0)),
            scratch_shapes=[pltpu.VMEM((2, MOE_BLOCK, D), F32), pltpu.VMEM((D, F), BF16),
                            pltpu.VMEM((D, F), BF16), pltpu.SemaphoreType.DMA((2,))]),
        out_shape=jax.ShapeDtypeStruct((n_slots, F), BF16),
        compiler_params=_params("arbitrary"),
    )(block_exp, slot_tok, h2_all, P['w_gate_e'], P['w_up_e'])

    y_slots = pl.pallas_call(
        _moe_down_kernel,
        grid_spec=pltpu.PrefetchScalarGridSpec(
            num_scalar_prefetch=1,
            grid=(n_blocks,),
            in_specs=[pl.BlockSpec((MOE_BLOCK, F), lambda b, be: (b, 0)),
                      pl.BlockSpec((1, F, D), lambda b, be: (be[b], 0, 0)),
                      pl.BlockSpec((MOE_BLOCK, 1), lambda b, be: (b, 0))],
            out_specs=pl.BlockSpec((MOE_BLOCK, D), lambda b, be: (b, 0)),
            scratch_shapes=[pltpu.VMEM((F, D), BF16)]),
        out_shape=jax.ShapeDtypeStruct((n_slots, D), F32),
        compiler_params=_params("arbitrary"),
    )(block_exp, a_all, P['w_down_e'], slot_w.reshape(n_slots, 1))

    y_all = pl.pallas_call(
        _combine_kernel,
        grid_spec=pltpu.PrefetchScalarGridSpec(
            num_scalar_prefetch=2,
            grid=(R // MG,),
            in_specs=[pl.BlockSpec(memory_space=pl.ANY),
                      pl.BlockSpec((MG, D), lambda i, d0, d1: (i, 0)),
                      pl.BlockSpec((1, MG, D),
                                   lambda i, d0, d1: (jnp.where(i * MG < RP, (i * MG) // S, B), 0, 5))],
            out_specs=pl.BlockSpec((MG, D), lambda i, d0, d1: (i, 0)),
            scratch_shapes=[pltpu.VMEM((2, TOP_K, MG, D), F32), pltpu.SemaphoreType.DMA((2,))]),
        out_shape=jax.ShapeDtypeStruct((R, D), F32),
        compiler_params=_params("arbitrary"),
    )(dest[:, 0], dest[:, 1], y_slots, x1_all, mod_all)

    return y_all, new_conv_p, new_conv_s, ckv_all, kpe_all


def kernel(x_prompt, x_sample, state_conv, cache_ckv, cache_kpe, page_table, c_prompt, c_sample, w_ada, b_ada, g_norm_mix, g_norm_ffn, w_in, w_conv, w_conv_out, g_q_lat, w_q_up, g_kv_lat, w_uk, w_uv, g_qk_q, g_qk_k, w_attn_out, w_o, w_route_group, b_route_group, w_route_expert, b_route_expert, w_gate_e, w_up_e, w_down_e):
    B, S, D = x_prompt.shape
    NB, T, _ = x_sample.shape
    depth = w_in.shape[0]
    RP = B * S
    KVR = cache_ckv.shape[-1]
    DR = cache_kpe.shape[-1]
    C = w_conv.shape[-1]
    x_all = jnp.concatenate([x_prompt.reshape(RP, D), x_sample.transpose(1, 0, 2).reshape(T * NB, D)], axis=0)
    c_rows = jnp.concatenate([jnp.repeat(c_prompt, NB, axis=0), c_sample], axis=0)
    weights = dict(w_ada=w_ada, b_ada=b_ada, g_norm_mix=g_norm_mix, g_norm_ffn=g_norm_ffn, w_in=w_in,
                   w_conv=w_conv, w_conv_out=w_conv_out, g_q_lat=g_q_lat, w_q_up=w_q_up, g_kv_lat=g_kv_lat,
                   w_uk=w_uk, w_uv=w_uv, g_qk_q=g_qk_q, g_qk_k=g_qk_k, w_attn_out=w_attn_out, w_o=w_o,
                   w_route_group=w_route_group, b_route_group=b_route_group, w_route_expert=w_route_expert,
                   b_route_expert=b_route_expert, w_gate_e=w_gate_e, w_up_e=w_up_e, w_down_e=w_down_e)
    conv_p, conv_s, ckv_p, kpe_p, ckv_s, kpe_s = [], [], [], [], [], []
    for l in range(depth):
        P = {k: v[l] for k, v in weights.items()}
        mod_all = _adaln(c_rows, P['w_ada'], P['b_ada']).reshape(B + 1, NB, 6 * D)
        x_all, cp, cs, ckv_all, kpe_all = _layer(x_all, (B, S, NB, T), mod_all, state_conv[l], cache_ckv,
                                                 cache_kpe, page_table, l, P)
        conv_p.append(cp)
        conv_s.append(cs)
        ckv_p.append(ckv_all[:RP].reshape(B, S, KVR))
        kpe_p.append(kpe_all[:RP].reshape(B, S, DR))
        ckv_s.append(ckv_all[RP:].reshape(T, NB, KVR).transpose(1, 0, 2))
        kpe_s.append(kpe_all[RP:].reshape(T, NB, DR).transpose(1, 0, 2))
    y_prompt = x_all[:RP].reshape(B, S, D)
    y_sample = x_all[RP:].reshape(T, NB, D).transpose(1, 0, 2)
    return (y_prompt, y_sample, jnp.stack(conv_p), jnp.stack(conv_s), jnp.stack(ckv_p), jnp.stack(kpe_p),
            jnp.stack(ckv_s), jnp.stack(kpe_s))
```

```python
import functools

import jax
import jax.numpy as jnp
import numpy as np
from jax import lax
from jax.experimental import pallas as pl
from jax.experimental.pallas import tpu as pltpu

EPS = 1e-6
ROPE_BASE = 10000.0
NEG_INF = -1e30
TOP_K = 2
BF16 = jnp.bfloat16
F32 = jnp.float32
VMEM_LIMIT_BYTES = 56 * 1024 * 1024
LANES = 128
SUBLANES = 8
ROW_TILE = 512
MOE_BLOCK = 128
PAGES_PER_STEP = 8


def _params(*sem):
    return pltpu.CompilerParams(dimension_semantics=sem, vmem_limit_bytes=VMEM_LIMIT_BYTES)


def _divisor_tile(n, pref, mult):
    best = None
    t = mult
    while t <= min(n, pref):
        if n % t == 0:
            best = t
        t += mult
    return best if best is not None else n


def _dot(a, b):
    return jnp.dot(a, b, preferred_element_type=F32)


def _dot_nt(a, b):
    return lax.dot_general(a, b, (((1,), (1,)), ((), ())), preferred_element_type=F32)


def _sigmoid(x):
    return 1.0 / (1.0 + jnp.exp(-x))


def _rms(x):
    return x * lax.rsqrt(jnp.mean(x * x, axis=-1, keepdims=True) + EPS)


def _expand_mod(m, rows):
    reps = rows // m.shape[0]
    return m if reps == 1 else jnp.concatenate([m] * reps, axis=0)


def _rope_rows(x, cos2, sin2):
    half = x.shape[1] // 2
    swapped = jnp.concatenate([x[:, half:], x[:, :half]], axis=1)
    return x * cos2 + swapped * sin2


def _ada_kernel(c_ref, w_ref, b_ref, o_ref):
    c = c_ref[...]
    s = (c * _sigmoid(c)).astype(BF16)
    o_ref[...] = _dot(s, w_ref[...].astype(BF16)) + b_ref[...]


def _adaln(c_rows, w_ada, b_ada):
    r, d = c_rows.shape
    n = w_ada.shape[1]
    tn = _divisor_tile(n, 1024, LANES)
    return pl.pallas_call(
        _ada_kernel,
        grid=(n // tn,),
        in_specs=[pl.BlockSpec((r, d), lambda j: (0, 0)),
                  pl.BlockSpec((d, tn), lambda j: (0, j)),
                  pl.BlockSpec((1, tn), lambda j: (0, j))],
        out_specs=pl.BlockSpec((r, tn), lambda j: (0, j)),
        out_shape=jax.ShapeDtypeStruct((r, n), F32),
        compiler_params=_params("parallel"),
    )(c_rows, w_ada, b_ada.reshape(1, n))


def _norm_kernel(x_ref, g_ref, sc_ref, sh_ref, o_ref):
    x = x_ref[...]
    rows = x.shape[0]
    y = _rms(x) * g_ref[...]
    y = y * (1.0 + _expand_mod(sc_ref[0], rows)) + _expand_mod(sh_ref[0], rows)
    o_ref[...] = y.astype(o_ref.dtype)


def _conv_kernel(h_ref, wh_ref, wb_ref, wc_ref, wconv_ref, hist_ref, z_ref, st_ref, carry_ref, *,
                 tiles_per_seq, n_prompt_tiles, nb):
    i = pl.program_id(0)
    j = pl.program_id(1)
    h = h_ref[...]
    u = _dot(h, wc_ref[...]) * _dot(h, wh_ref[...])
    bg = _dot(h, wb_ref[...])
    tm = u.shape[0]
    w = wconv_ref[...]

    @pl.when(i < n_prompt_tiles)
    def _():
        @pl.when(i % tiles_per_seq == 0)
        def _():
            carry_ref[j] = jnp.zeros(carry_ref.shape[1:], F32)

        tail = carry_ref[j]
        row8 = lax.broadcasted_iota(jnp.int32, (SUBLANES, 1), 0)

        def shifted(k):
            ur = pltpu.roll(u, k, 0)
            top = jnp.where(row8 < k, pltpu.roll(tail, k, 0), ur[:SUBLANES])
            return jnp.concatenate([top, ur[SUBLANES:]], axis=0)

        y = u * w[2:3] + shifted(2) * w[0:1] + shifted(1) * w[1:2]
        z_ref[...] = (bg * y).astype(z_ref.dtype)
        carry_ref[j] = u[tm - SUBLANES:]

    @pl.when(i >= n_prompt_tiles)
    def _():
        ext = jnp.concatenate([hist_ref[...], u], axis=0)
        y = ext[2 * nb:] * w[2:3] + ext[:tm] * w[0:1] + ext[nb:nb + tm] * w[1:2]
        z_ref[...] = (bg * y).astype(z_ref.dtype)

    st_ref[0] = u[tm - 2 * nb:]


def _q_kernel(h_ref, wqd_ref, gq_ref, wup_ref, gn_ref, gp_ref, cos_ref, sin_ref, qn_ref, qp_ref, *,
              n_heads, d_nope, d_rope, scale):
    qd = _dot(h_ref[...], wqd_ref[...])
    qdn = (_rms(qd) * gq_ref[...]).astype(BF16)
    qu = _dot(qdn, wup_ref[...])
    cos2 = cos_ref[...]
    sin2 = sin_ref[...]
    d_qk = d_nope + d_rope
    for hh in range(n_heads):
        qn = qu[:, hh * d_nope:(hh + 1) * d_nope]
        qp = qu[:, n_heads * d_nope + hh * d_rope:n_heads * d_nope + (hh + 1) * d_rope]
        ss = jnp.sum(qn * qn, axis=-1, keepdims=True) + jnp.sum(qp * qp, axis=-1, keepdims=True)
        r = lax.rsqrt(ss / d_qk + EPS)
        qn = qn * r * gn_ref[...]
        qp = _rope_rows(qp * r * gp_ref[...], cos2, sin2)
        qn_ref[:, hh * d_nope:(hh + 1) * d_nope] = (qn * scale).astype(qn_ref.dtype)
        qp_ref[hh] = (qp * scale).astype(qp_ref.dtype)


def _rope_cols(x, cos_t, sin_t):
    half = x.shape[0] // 2
    x1, x2 = x[:half], x[half:]
    return jnp.concatenate([x1 * cos_t - x2 * sin_t, x2 * cos_t + x1 * sin_t], axis=0)


def _kv_kernel(h_ref, wkvd_ref, wkpe_ref, wkpet_ref, gkv_ref, wukt_ref, wuv_ref, gn_ref, gp_ref, cos_ref, sin_ref,
               ckv_ref, kpe_ref, kt_ref, v_ref, *, n_heads, d_nope, d_rope):
    h = h_ref[...]
    ckv = _rms(_dot(h, wkvd_ref[...])) * gkv_ref[...]
    ckv_ref[...] = ckv
    kpe_ref[...] = _dot(h, wkpe_ref[...])
    cb = ckv.astype(BF16)
    v_ref[...] = _dot(cb, wuv_ref[...]).astype(v_ref.dtype)
    knt = _dot_nt(wukt_ref[...], cb)
    kpt = _dot_nt(wkpet_ref[...], h)
    sp = jnp.sum(kpt * kpt, axis=0, keepdims=True)
    krt = _rope_cols(kpt * gp_ref[...], cos_ref[...], sin_ref[...])
    d_qk = d_nope + d_rope
    for hh in range(n_heads):
        kn = knt[hh * d_nope:(hh + 1) * d_nope]
        r = lax.rsqrt((jnp.sum(kn * kn, axis=0, keepdims=True) + sp) / d_qk + EPS)
        kt_ref[hh, 0, :d_nope, :] = (kn * r * gn_ref[...]).astype(kt_ref.dtype)
        kt_ref[hh, 0, d_nope:, :] = (krt * r).astype(kt_ref.dtype)


def _flash_kernel(qn_ref, qp_ref, kt_ref, v_ref, o_ref, vext_ref, m_ref, acc_ref, *, tq):
    qi = pl.program_id(2)
    dv = v_ref.shape[1]

    @pl.when(qi == 0)
    def _():
        vext_ref[:, :dv] = v_ref[...]
        vext_ref[:, dv:] = jnp.ones((vext_ref.shape[0], vext_ref.shape[1] - dv), vext_ref.dtype)

    q = jnp.concatenate([qn_ref[...], qp_ref[0]], axis=1)
    m_ref[...] = jnp.full(m_ref.shape, NEG_INF, F32)
    acc_ref[...] = jnp.zeros(acc_ref.shape, F32)
    reps_s = tq // m_ref.shape[1]
    reps_a = acc_ref.shape[1] // m_ref.shape[1]

    def chunk(c, diagonal):
        s = _dot(q, kt_ref[0, c])
        if diagonal:
            row = lax.broadcasted_iota(jnp.int32, s.shape, 0)
            col = lax.broadcasted_iota(jnp.int32, s.shape, 1)
            s = jnp.where(col <= row, s, NEG_INF)
        m_prev = m_ref[...]
        m_new = jnp.maximum(m_prev, jnp.max(s, axis=-1, keepdims=True))
        a = jnp.exp(m_prev - m_new)
        p = jnp.exp(s - jnp.concatenate([m_new] * reps_s, axis=1))
        rows = pl.ds(pl.multiple_of(c * tq, tq), tq)
        acc_ref[...] = (jnp.concatenate([a] * reps_a, axis=1) * acc_ref[...]
                        + _dot(p.astype(BF16), vext_ref[rows, :]))
        m_ref[...] = m_new

    def body(c, carry):
        chunk(c, False)
        return carry

    lax.fori_loop(0, qi, body, 0)
    chunk(qi, True)
    acc = acc_ref[...]
    o_ref[...] = (acc[:, :dv] / acc[:, dv:2 * dv]).astype(o_ref.dtype)


def _qabs_kernel(qn_ref, gn_ref, wuk_ref, o_ref):
    q = (qn_ref[...].astype(F32) * gn_ref[...]).astype(BF16)
    o_ref[0] = _dot_nt(q, wuk_ref[...]).astype(o_ref.dtype)


def _paged_kernel(pt_ref, qt_ref, qpe_ref, *refs, pps, n_heads, d_qk, t_new):
    del pt_ref
    ckv_refs = refs[:pps]
    kpe_refs = refs[pps:2 * pps]
    (cos_ref, sin_ref, cn_ref, kn_ref, cosn_ref, sinn_ref, maskn_ref, wukt_ref, gp_ref,
     o_ref, m_ref, l_ref, acc_ref, cbp_ref, krtp_ref, rinvp_ref) = refs[2 * pps:]
    p = pl.program_id(1)

    @pl.when(p == 0)
    def _():
        m_ref[...] = jnp.full(m_ref.shape, NEG_INF, F32)
        l_ref[...] = jnp.zeros(l_ref.shape, F32)
        acc_ref[...] = jnp.zeros(acc_ref.shape, F32)
        cbp_ref[...] = jnp.zeros(cbp_ref.shape, cbp_ref.dtype)
        krtp_ref[...] = jnp.zeros(krtp_ref.shape, krtp_ref.dtype)
        rinvp_ref[...] = jnp.zeros(rinvp_ref.shape, F32)

    qt = qt_ref[0]
    qpe = qpe_ref[0]
    d_nope = wukt_ref.shape[0] // n_heads

    def key_stats(cb, kpt, cos_t, sin_t):
        nk = cb.shape[0]
        kt = _dot_nt(wukt_ref[...], cb)
        ssn = jnp.sum((kt * kt).reshape(n_heads, d_nope, nk), axis=1)
        ssp = jnp.sum(kpt * kpt, axis=0, keepdims=True)
        rinv = lax.rsqrt((ssn + ssp) / d_qk + EPS)
        krt = _rope_cols(kpt * gp_ref[...], cos_t, sin_t).astype(BF16)
        return rinv, krt

    def scores(cb, krt, rinv):
        return (_dot_nt(qt, cb) + _dot(qpe, krt)) * jnp.concatenate([rinv] * t_new, axis=0)

    def accumulate(s, cb, weight=None):
        m_prev = m_ref[...]
        m_new = jnp.maximum(m_prev, jnp.max(s, axis=-1, keepdims=True))
        a = jnp.exp(m_prev - m_new)
        e = jnp.exp(s - m_new)
        if weight is not None:
            e = e * weight
        l_ref[...] = a * l_ref[...] + jnp.sum(e, axis=-1, keepdims=True)
        acc_ref[...] = a * acc_ref[...] + _dot(e.astype(BF16), cb)
        m_ref[...] = m_new

    cb_prev = cbp_ref[...]
    s_prev = scores(cb_prev, krtp_ref[...], rinvp_ref[...])
    cb = jnp.concatenate([r[0, 0] for r in ckv_refs], axis=0).astype(BF16)
    kpt = jnp.concatenate([r[0, 0] for r in kpe_refs], axis=1)
    rinv, krt = key_stats(cb, kpt, cos_ref[...], sin_ref[...])
    accumulate(jnp.where(p > 0, s_prev, NEG_INF), cb_prev, jnp.where(p > 0, 1.0, 0.0))
    cbp_ref[...] = cb
    krtp_ref[...] = krt
    rinvp_ref[...] = rinv

    @pl.when(p == pl.num_programs(1) - 1)
    def _():
        accumulate(scores(cb, krt, rinv), cb)
        pad = maskn_ref.shape[1] - cn_ref.shape[1]
        cb_new = jnp.concatenate([cn_ref[0], jnp.zeros((pad, cn_ref.shape[2]), F32)], axis=0).astype(BF16)
        rinv_new, krt_new = key_stats(cb_new, kn_ref[0], cosn_ref[...], sinn_ref[...])
        s_new = jnp.where(maskn_ref[...] > 0.0, scores(cb_new, krt_new, rinv_new), NEG_INF)
        accumulate(s_new, cb_new)
        o_ref[0] = (acc_ref[...] / l_ref[...]).astype(o_ref.dtype)


def _ouv_kernel(ol_ref, wuv_ref, o_ref):
    o_ref[...] = _dot(ol_ref[0], wuv_ref[...]).astype(o_ref.dtype)


def _mix_kernel(h_ref, z_ref, op_ref, os_ref, wga_ref, wgb_ref, wco_ref, wao_ref, m_ref, *, n_prompt_tiles):
    h = h_ref[...]
    ga = _dot(h, wga_ref[...])
    gb = _dot(h, wgb_ref[...])
    y_conv = _dot(z_ref[...], wco_ref[...])
    o = jnp.where(pl.program_id(0) < n_prompt_tiles, op_ref[...], os_ref[...])
    y_attn = _dot(o, wao_ref[...])
    m_ref[...] = (_sigmoid(ga) * y_conv + _sigmoid(gb) * y_attn).astype(m_ref.dtype)


def _wo_kernel(m_ref, wo_ref, x_ref, gt_ref, g_ref, sc_ref, sh_ref, wr_ref, br_ref, gid_ref,
               x1_ref, h2_ref, eid_ref, ew_ref, cnt_ref, run_ref, *, n_groups):
    rows = x_ref.shape[0]

    @pl.when(pl.program_id(0) == 0)
    def _():
        run_ref[...] = jnp.zeros(run_ref.shape, F32)

    x1 = x_ref[...] + _expand_mod(gt_ref[0], rows) * _dot(m_ref[...], wo_ref[...])
    x1_ref[...] = x1
    h2 = _rms(x1) * g_ref[...]
    h2 = h2 * (1.0 + _expand_mod(sc_ref[0], rows)) + _expand_mod(sh_ref[0], rows)
    h2_ref[...] = h2
    logits = _dot(h2.astype(BF16), wr_ref[...]) + br_ref[...]
    icol = lax.broadcasted_iota(jnp.int32, logits.shape, 1)
    col = icol.astype(F32)
    gid = gid_ref[...]
    big = float(logits.shape[1])
    lg = jnp.where(col < n_groups, logits, NEG_INF)
    mg = jnp.max(lg, axis=-1, keepdims=True)
    g_star = jnp.min(jnp.where(lg == mg, col, big), axis=-1, keepdims=True)
    p_top = 1.0 / jnp.sum(jnp.exp(lg - mg), axis=-1, keepdims=True)
    le = jnp.where(gid == g_star, logits, NEG_INF)
    v1 = jnp.max(le, axis=-1, keepdims=True)
    i1 = jnp.min(jnp.where(le == v1, col, big), axis=-1, keepdims=True)
    le2 = jnp.where(col == i1, NEG_INF, le)
    v2 = jnp.max(le2, axis=-1, keepdims=True)
    i2 = jnp.min(jnp.where(le2 == v2, col, big), axis=-1, keepdims=True)
    e2 = jnp.exp(v2 - v1)
    w1 = p_top * (1.0 / (1.0 + e2))
    w2 = p_top * (e2 / (1.0 + e2))
    oh1 = col == i1
    oh2 = col == i2
    onehot = jnp.where(jnp.logical_or(oh1, oh2), 1.0, 0.0)
    r_io = lax.broadcasted_iota(jnp.int32, (rows, rows), 0)
    c_io = lax.broadcasted_iota(jnp.int32, (rows, rows), 1)
    tri = jnp.where(c_io < r_io, 1.0, 0.0).astype(BF16)
    before = _dot(tri, onehot.astype(BF16)) + run_ref[...]
    rank1 = jnp.sum(jnp.where(oh1, before, 0.0), axis=-1, keepdims=True)
    rank2 = jnp.sum(jnp.where(oh2, before, 0.0), axis=-1, keepdims=True)
    run_ref[...] = run_ref[...] + jnp.sum(onehot, axis=0, keepdims=True)
    cnt_ref[...] = jnp.broadcast_to(run_ref[...], cnt_ref.shape)
    meta = jnp.where(icol == 0, i1 - n_groups, jnp.where(icol == 1, i2 - n_groups,
                     jnp.where(icol == 2, rank1, jnp.where(icol == 3, rank2, 0.0))))
    eid_ref[...] = meta.astype(jnp.int32)
    ew_ref[...] = jnp.where(icol == 0, w1, jnp.where(icol == 1, w2, 0.0))


def _row_gather(idx_ref, base, n, src_hbm, dst, sem):
    def body(r, carry):
        tok = idx_ref[base + r]
        pltpu.make_async_copy(src_hbm.at[pl.ds(tok, 1), :], dst.at[pl.ds(r, 1), :], sem).start()
        return carry
    lax.fori_loop(0, n, body, 0)


def _rows_wait(n, src_hbm, dst, sem):
    pltpu.make_async_copy(src_hbm.at[pl.ds(0, n), :], dst, sem).wait()


def _expert_changed(be_ref, b):
    return jnp.logical_or(b == 0, be_ref[b] != be_ref[jnp.maximum(b - 1, 0)])


def _moe_up_kernel(be_ref, tok_ref, h_hbm, wg_ref, wu_ref, a_ref, xbuf, wg_bf, wu_bf, sem):
    b = pl.program_id(0)
    nb = pl.num_programs(0)
    blk = xbuf.shape[1]
    slot = b % 2

    @pl.when(_expert_changed(be_ref, b))
    def _():
        wg_bf[...] = wg_ref[0].astype(BF16)
        wu_bf[...] = wu_ref[0].astype(BF16)

    @pl.when(b == 0)
    def _():
        _row_gather(tok_ref, 0, blk, h_hbm, xbuf.at[0], sem.at[0])

    @pl.when(b + 1 < nb)
    def _():
        _row_gather(tok_ref, (b + 1) * blk, blk, h_hbm, xbuf.at[1 - slot], sem.at[1 - slot])

    _rows_wait(blk, h_hbm, xbuf.at[slot], sem.at[slot])
    x = xbuf[slot].astype(BF16)
    g = _dot(x, wg_bf[...])
    u = _dot(x, wu_bf[...])
    a_ref[...] = (g * _sigmoid(g) * u).astype(a_ref.dtype)


def _moe_down_kernel(be_ref, a_ref, wd_ref, y_ref, wd_bf):
    @pl.when(_expert_changed(be_ref, pl.program_id(0)))
    def _():
        wd_bf[...] = wd_ref[0].astype(BF16)

    y_ref[...] = _dot(a_ref[...], wd_bf[...])


def _combine_kernel(d0_ref, d1_ref, y_hbm, x1_ref, gt_ref, ew_ref, o_ref, ybuf, sem):
    i = pl.program_id(0)
    n = pl.num_programs(0)
    rows = x1_ref.shape[0]
    slot = i % 2

    def start(step, s):
        _row_gather(d0_ref, step * rows, rows, y_hbm, ybuf.at[s, 0], sem.at[s])
        _row_gather(d1_ref, step * rows, rows, y_hbm, ybuf.at[s, 1], sem.at[s])

    @pl.when(i == 0)
    def _():
        start(0, 0)

    @pl.when(i + 1 < n)
    def _():
        start(i + 1, 1 - slot)

    _rows_wait(rows, y_hbm, ybuf.at[slot, 0], sem.at[slot])
    _rows_wait(rows, y_hbm, ybuf.at[slot, 1], sem.at[slot])
    ew = ew_ref[...]
    y = ybuf[slot, 0] * ew[:, 0:1] + ybuf[slot, 1] * ew[:, 1:2]
    o_ref[...] = x1_ref[...] + _expand_mod(gt_ref[0], rows) * y


def _rope_tables(pos, d_rope):
    half = d_rope // 2
    inv = ROPE_BASE ** (-jnp.arange(half, dtype=F32) / half)
    ang = pos.astype(F32)[:, None] * inv[None, :]
    cos, sin = jnp.cos(ang), jnp.sin(ang)
    return jnp.concatenate([cos, cos], axis=1), jnp.concatenate([-sin, sin], axis=1)


def _layer(x_all, dims, mod_all, state_conv_l, cache_ckv, cache_kpe, page_table, l, P):
    (B, S, NB, T) = dims
    R, D = x_all.shape
    RP, RS = B * S, T * NB
    C = P['w_conv'].shape[1]
    QR = P['g_q_lat'].shape[0]
    KVR = P['g_kv_lat'].shape[0]
    H, DN = P['w_uk'].shape[1], P['w_uk'].shape[2]
    DV = P['w_uv'].shape[2]
    DR = cache_kpe.shape[-1]
    DQK = DN + DR
    scale = DQK ** -0.5
    PAGE = cache_ckv.shape[2]
    NPAGES = page_table.shape[1]
    PAST = NPAGES * PAGE
    G = P['w_route_group'].shape[1]
    E = P['w_route_expert'].shape[1]
    F = P['w_gate_e'].shape[2]
    MG = NB

    tm = _divisor_tile(int(np.gcd(S, RS)), ROW_TILE, MG)
    n_tiles, np_tiles = R // tm, RP // tm

    def mod_spec(chunk):
        def idx(i, *_):
            return (jnp.where(i < np_tiles, (i * tm) // S, B), 0, chunk)
        return pl.BlockSpec((1, MG, D), idx)

    def row_spec(width, tile=tm):
        return pl.BlockSpec((tile, width), lambda i, *_: (i, 0))

    def full_spec(shape):
        nd = len(shape)
        return pl.BlockSpec(shape, lambda *_: (0,) * nd)

    offs = np.cumsum([0, C, C, C, QR, KVR, DR, D, D])
    w_h, w_b, w_c, w_qd, w_kvd, w_kpe, w_ga, w_gb = [
        P['w_in'][:, offs[k]:offs[k + 1]].astype(BF16) for k in range(8)]
    w_up = P['w_q_up'].reshape(QR, H, DQK)
    w_up = jnp.concatenate([w_up[:, :, :DN].reshape(QR, H * DN), w_up[:, :, DN:].reshape(QR, H * DR)],
                           axis=1).astype(BF16)
    w_uk = P['w_uk'].reshape(KVR, H * DN).astype(BF16)
    w_uv = P['w_uv'].reshape(KVR, H * DV).astype(BF16)
    w_co = P['w_conv_out'].astype(BF16)
    w_ao = P['w_attn_out'].astype(BF16)
    w_o = P['w_o'].astype(BF16)
    gq_n, gq_p = P['g_qk_q'][:DN].reshape(1, DN), P['g_qk_q'][DN:].reshape(1, DR)
    gk_n, gk_p = P['g_qk_k'][:DN].reshape(1, DN), P['g_qk_k'][DN:].reshape(1, DR)

    pos_rows = jnp.concatenate([jnp.tile(jnp.arange(S), B), PAST + jnp.repeat(jnp.arange(T), NB)])
    cos_rows, sin_rows = _rope_tables(pos_rows, DR)

    h1 = pl.pallas_call(
        _norm_kernel,
        grid=(n_tiles,),
        in_specs=[row_spec(D), full_spec((1, D)), mod_spec(1), mod_spec(0)],
        out_specs=row_spec(D),
        out_shape=jax.ShapeDtypeStruct((R, D), BF16),
        compiler_params=_params("parallel"),
    )(x_all, P['g_norm_mix'].reshape(1, D), mod_all, mod_all)

    assert T >= 2 and S % RS == 0 and RS >= 2 * NB
    tc = _divisor_tile(C, 512, LANES)
    n_c = C // tc
    tiles_per_seq = S // RS
    n_ct = R // RS
    hist_t = state_conv_l.transpose(1, 0, 2).reshape(2 * NB, C)
    z_all, st_all = pl.pallas_call(
        functools.partial(_conv_kernel, tiles_per_seq=tiles_per_seq, n_prompt_tiles=RP // RS, nb=NB),
        grid=(n_ct, n_c),
        in_specs=[pl.BlockSpec((RS, D), lambda i, j: (i, 0)),
                  pl.BlockSpec((D, tc), lambda i, j: (0, j)),
                  pl.BlockSpec((D, tc), lambda i, j: (0, j)),
                  pl.BlockSpec((D, tc), lambda i, j: (0, j)),
                  pl.BlockSpec((3, tc), lambda i, j: (0, j)),
                  pl.BlockSpec((2 * NB, tc), lambda i, j: (0, j))],
        out_specs=[pl.BlockSpec((RS, tc), lambda i, j: (i, j)),
                   pl.BlockSpec((1, 2 * NB, tc), lambda i, j: (i, 0, j))],
        out_shape=[jax.ShapeDtypeStruct((R, C), BF16), jax.ShapeDtypeStruct((n_ct, 2 * NB, C), F32)],
        scratch_shapes=[pltpu.VMEM((n_c, SUBLANES, tc), F32)],
        compiler_params=_params("arbitrary", "arbitrary"),
    )(h1, w_h, w_b, w_c, P['w_conv'], hist_t)
    new_conv_p = st_all[tiles_per_seq - 1:RP // RS:tiles_per_seq, 2 * NB - 2:]
    new_conv_s = st_all[n_ct - 1].reshape(2, NB, C).transpose(1, 0, 2)
    sblk = RP // RS

    qn_all, qp_all = pl.pallas_call(
        functools.partial(_q_kernel, n_heads=H, d_nope=DN, d_rope=DR, scale=scale),
        grid=(n_tiles,),
        in_specs=[row_spec(D), full_spec((D, QR)), full_spec((1, QR)), full_spec((QR, H * DQK)),
                  full_spec((1, DN)), full_spec((1, DR)), row_spec(DR), row_spec(DR)],
        out_specs=[row_spec(H * DN), pl.BlockSpec((H, tm, DR), lambda i: (0, i, 0))],
        out_shape=[jax.ShapeDtypeStruct((R, H * DN), BF16), jax.ShapeDtypeStruct((H, R, DR), BF16)],
        compiler_params=_params("parallel"),
    )(h1, w_qd, P['g_q_lat'].reshape(1, QR), w_up, gq_n, gq_p, cos_rows, sin_rows)

    half = DR // 2
    w_ukt = w_uk.T
    gk_nc, gk_pc = gk_n.reshape(DN, 1), gk_p.reshape(DR, 1)
    cos_t = cos_rows[:, :half].T
    sin_t = sin_rows[:, half:].T
    ckv_all, kpe_all, kt_all, v_all = pl.pallas_call(
        functools.partial(_kv_kernel, n_heads=H, d_nope=DN, d_rope=DR),
        grid=(n_tiles,),
        in_specs=[row_spec(D), full_spec((D, KVR)), full_spec((D, DR)), full_spec((DR, D)), full_spec((1, KVR)),
                  full_spec((H * DN, KVR)), full_spec((KVR, H * DV)), full_spec((DN, 1)), full_spec((DR, 1)),
                  pl.BlockSpec((half, tm), lambda i: (0, i)), pl.BlockSpec((half, tm), lambda i: (0, i))],
        out_specs=[row_spec(KVR), row_spec(DR),
                   pl.BlockSpec((H, 1, DQK, tm), lambda i: (0, i, 0, 0)), row_spec(H * DV)],
        out_shape=[jax.ShapeDtypeStruct((R, KVR), F32), jax.ShapeDtypeStruct((R, DR), F32),
                   jax.ShapeDtypeStruct((H, n_tiles, DQK, tm), BF16),
                   jax.ShapeDtypeStruct((R, H * DV), BF16)],
        compiler_params=_params("parallel"),
    )(h1, w_kvd, w_kpe, w_kpe.T, P['g_kv_lat'].reshape(1, KVR), w_ukt, w_uv, gk_nc, gk_pc, cos_t, sin_t)

    tq = tm
    nq = S // tq
    o_p = pl.pallas_call(
        functools.partial(_flash_kernel, tq=tq),
        grid=(B, H, nq),
        in_specs=[pl.BlockSpec((tq, DN), lambda b, h, q: (b * nq + q, h)),
                  pl.BlockSpec((1, tq, DR), lambda b, h, q: (h, b * nq + q, 0)),
                  pl.BlockSpec((1, nq, DQK, tq), lambda b, h, q: (h, b, 0, 0)),
                  pl.BlockSpec((S, DV), lambda b, h, q: (b, h))],
        out_specs=pl.BlockSpec((tq, DV), lambda b, h, q: (b * nq + q, h)),
        out_shape=jax.ShapeDtypeStruct((RP, H * DV), BF16),
        scratch_shapes=[pltpu.VMEM((S, 2 * DV), BF16), pltpu.VMEM((tq, LANES), F32),
                        pltpu.VMEM((tq, 2 * DV), F32)],
        compiler_params=_params("parallel", "parallel", "arbitrary"),
    )(qn_all, qp_all, kt_all, v_all)

    qt = pl.pallas_call(
        _qabs_kernel,
        grid=(H,),
        in_specs=[pl.BlockSpec((RS, DN), lambda h: (sblk, h)), full_spec((1, DN)),
                  pl.BlockSpec((KVR, DN), lambda h: (0, h))],
        out_specs=pl.BlockSpec((1, RS, KVR), lambda h: (h, 0, 0)),
        out_shape=jax.ShapeDtypeStruct((H, RS, KVR), BF16),
        compiler_params=_params("parallel"),
    )(qn_all, gk_n, w_uk)
    QROWS = T * H
    qt = qt.reshape(H, T, NB, KVR).transpose(2, 1, 0, 3).reshape(NB, QROWS, KVR)
    qpe = qp_all[:, RP:].reshape(H, T, NB, DR).transpose(2, 1, 0, 3).reshape(NB, QROWS, DR)
    pos_keys = jnp.arange(PAST + PAGE)
    cos_k, sin_k = _rope_tables(pos_keys, DR)
    cos_kt, sin_kt = cos_k[:, :half].T, sin_k[:, half:].T
    t_pad = -(-T // SUBLANES) * SUBLANES
    ckv_s = ckv_all[RP:].reshape(T, NB, KVR).transpose(1, 0, 2)
    kpe_s = kpe_all[RP:].reshape(T, NB, DR).transpose(1, 0, 2)
    cn = jnp.pad(ckv_s, ((0, 0), (0, t_pad - T), (0, 0)))
    knt = jnp.pad(kpe_s.transpose(0, 2, 1), ((0, 0), (0, 0), (0, PAGE - T)))
    key_j = jnp.arange(PAGE)[None, :]
    row_t = (jnp.arange(QROWS) // H)[:, None]
    mask_new = ((key_j <= row_t) & (key_j < T)).astype(F32)
    cache_kpe_t = jnp.swapaxes(cache_kpe, 2, 3)
    pps = _divisor_tile(NPAGES, PAGES_PER_STEP, 1)
    n_steps = NPAGES // pps

    def page_spec(k, shape):
        return pl.BlockSpec((1, 1) + shape, lambda b, p, pt: (l, pt[b, p * pps + k], 0, 0))

    def cfull(shape):
        nd = len(shape)
        return pl.BlockSpec(shape, lambda b, p, pt: (0,) * nd)

    o_lat = pl.pallas_call(
        functools.partial(_paged_kernel, pps=pps, n_heads=H, d_qk=DQK, t_new=T),
        grid_spec=pltpu.PrefetchScalarGridSpec(
            num_scalar_prefetch=1,
            grid=(NB, n_steps),
            in_specs=([pl.BlockSpec((1, QROWS, KVR), lambda b, p, pt: (b, 0, 0)),
                       pl.BlockSpec((1, QROWS, DR), lambda b, p, pt: (b, 0, 0))]
                      + [page_spec(k, (PAGE, KVR)) for k in range(pps)]
                      + [page_spec(k, (DR, PAGE)) for k in range(pps)]
                      + [pl.BlockSpec((half, pps * PAGE), lambda b, p, pt: (0, p)),
                         pl.BlockSpec((half, pps * PAGE), lambda b, p, pt: (0, p)),
                         pl.BlockSpec((1, t_pad, KVR), lambda b, p, pt: (b, 0, 0)),
                         pl.BlockSpec((1, DR, PAGE), lambda b, p, pt: (b, 0, 0)),
                         pl.BlockSpec((half, PAGE), lambda b, p, pt: (0, NPAGES)),
                         pl.BlockSpec((half, PAGE), lambda b, p, pt: (0, NPAGES)),
                         cfull((QROWS, PAGE)), cfull((H * DN, KVR)), cfull((DR, 1))]),
            out_specs=pl.BlockSpec((1, QROWS, KVR), lambda b, p, pt: (b, 0, 0)),
            scratch_shapes=[pltpu.VMEM((QROWS, 1), F32), pltpu.VMEM((QROWS, 1), F32),
                            pltpu.VMEM((QROWS, KVR), F32), pltpu.VMEM((pps * PAGE, KVR), BF16),
                            pltpu.VMEM((DR, pps * PAGE), BF16), pltpu.VMEM((H, pps * PAGE), F32)]),
        out_shape=jax.ShapeDtypeStruct((NB, QROWS, KVR), BF16),
        compiler_params=_params("parallel", "arbitrary"),
    )(page_table, qt, qpe, *([cache_ckv] * pps), *([cache_kpe_t] * pps), cos_kt, sin_kt,
      cn, knt, cos_kt, sin_kt, mask_new, w_ukt, gk_pc)
    o_lat = o_lat.reshape(NB, T, H, KVR).transpose(2, 1, 0, 3).reshape(H, RS, KVR)
    o_s = pl.pallas_call(
        _ouv_kernel,
        grid=(H,),
        in_specs=[pl.BlockSpec((1, RS, KVR), lambda h: (h, 0, 0)),
                  pl.BlockSpec((KVR, DV), lambda h: (0, h))],
        out_specs=pl.BlockSpec((RS, DV), lambda h: (0, h)),
        out_shape=jax.ShapeDtypeStruct((RS, H * DV), BF16),
        compiler_params=_params("parallel"),
    )(o_lat, w_uv)

    tn = _divisor_tile(D, 512, LANES)
    tmm = _divisor_tile(int(np.gcd(RP, RS)), 1024, SUBLANES)
    npm = RP // tmm
    m_all = pl.pallas_call(
        functools.partial(_mix_kernel, n_prompt_tiles=npm),
        grid=(R // tmm, D // tn),
        in_specs=[pl.BlockSpec((tmm, D), lambda i, j: (i, 0)),
                  pl.BlockSpec((tmm, C), lambda i, j: (i, 0)),
                  pl.BlockSpec((tmm, H * DV), lambda i, j: (jnp.minimum(i, npm - 1), 0)),
                  pl.BlockSpec((tmm, H * DV), lambda i, j: (jnp.maximum(i - npm, 0), 0)),
                  pl.BlockSpec((D, tn), lambda i, j: (0, j)),
                  pl.BlockSpec((D, tn), lambda i, j: (0, j)),
                  pl.BlockSpec((C, tn), lambda i, j: (0, j)),
                  pl.BlockSpec((H * DV, tn), lambda i, j: (0, j))],
        out_specs=pl.BlockSpec((tmm, tn), lambda i, j: (i, j)),
        out_shape=jax.ShapeDtypeStruct((R, D), BF16),
        compiler_params=_params("parallel", "parallel"),
    )(h1, z_all, o_p, o_s, w_ga, w_gb, w_co, w_ao)

    ncol = -(-(G + E) // LANES) * LANES
    w_r = jnp.pad(jnp.concatenate([P['w_route_group'], P['w_route_expert']], axis=1),
                  ((0, 0), (0, ncol - G - E))).astype(BF16)
    b_r = jnp.pad(jnp.concatenate([P['b_route_group'], P['b_route_expert']]), (0, ncol - G - E)).reshape(1, ncol)
    colv = np.arange(ncol)
    gid = np.where((colv >= G) & (colv < G + E), (colv - G) // (E // G), -1).astype(np.float32).reshape(1, ncol)
    x1_all, h2_all, meta_all, ew_all, cnt = pl.pallas_call(
        functools.partial(_wo_kernel, n_groups=G),
        grid=(n_tiles,),
        in_specs=[row_spec(D), full_spec((D, D)), row_spec(D), mod_spec(2), full_spec((1, D)),
                  mod_spec(4), mod_spec(3), full_spec((D, ncol)), full_spec((1, ncol)), full_spec((1, ncol))],
        out_specs=[row_spec(D), row_spec(D), row_spec(ncol), row_spec(ncol), full_spec((SUBLANES, ncol))],
        out_shape=[jax.ShapeDtypeStruct((R, D), F32), jax.ShapeDtypeStruct((R, D), F32),
                   jax.ShapeDtypeStruct((R, ncol), jnp.int32), jax.ShapeDtypeStruct((R, ncol), F32),
                   jax.ShapeDtypeStruct((SUBLANES, ncol), F32)],
        scratch_shapes=[pltpu.VMEM((1, ncol), F32)],
        compiler_params=_params("arbitrary"),
    )(m_all, w_o, x_all, mod_all, P['g_norm_ffn'].reshape(1, D), mod_all, mod_all, w_r, b_r, jnp.asarray(gid))

    nk = R * TOP_K
    eid = meta_all[:, :TOP_K]
    rank = meta_all[:, TOP_K:2 * TOP_K]
    counts = cnt[0, G:G + E].astype(jnp.int32)
    padded = (counts + MOE_BLOCK - 1) // MOE_BLOCK * MOE_BLOCK
    pad_end = jnp.cumsum(padded)
    pad_start = pad_end - padded
    dest = (pad_start[eid] + rank).astype(jnp.int32)
    n_blocks = (nk + E * (MOE_BLOCK - 1) + MOE_BLOCK - 1) // MOE_BLOCK
    n_slots = n_blocks * MOE_BLOCK
    flat_t = jnp.repeat(jnp.arange(R, dtype=jnp.int32), TOP_K)
    slot_tok = jnp.zeros((n_slots,), jnp.int32).at[dest.reshape(nk)].set(flat_t, unique_indices=True)
    block_start = jnp.arange(n_blocks, dtype=jnp.int32) * MOE_BLOCK
    block_exp = jnp.minimum(jnp.sum(pad_end[None, :] <= block_start[:, None], axis=1), E - 1).astype(jnp.int32)

    a_all = pl.pallas_call(
        _moe_up_kernel,
        grid_spec=pltpu.PrefetchScalarGridSpec(
            num_scalar_prefetch=2,
            grid=(n_blocks,),
            in_specs=[pl.BlockSpec(memory_space=pl.ANY),
                      pl.BlockSpec((1, D, F), lambda b, be, tok: (be[b], 0, 0)),
                      pl.BlockSpec((1, D, F), lambda b, be, tok: (be[b], 0, 0))],
            out_specs=pl.BlockSpec((MOE_BLOCK, F), lambda b, be, tok: (b, 0)),
            scratch_shapes=[pltpu.VMEM((2, MOE_BLOCK, D), F32), pltpu.VMEM((D, F), BF16),
                            pltpu.VMEM((D, F), BF16), pltpu.SemaphoreType.DMA((2,))]),
        out_shape=jax.ShapeDtypeStruct((n_slots, F), BF16),
        compiler_params=_params("arbitrary"),
    )(block_exp, slot_tok, h2_all, P['w_gate_e'], P['w_up_e'])

    y_slots = pl.pallas_call(
        _moe_down_kernel,
        grid_spec=pltpu.PrefetchScalarGridSpec(
            num_scalar_prefetch=1,
            grid=(n_blocks,),
            in_specs=[pl.BlockSpec((MOE_BLOCK, F), lambda b, be: (b, 0)),
                      pl.BlockSpec((1, F, D), lambda b, be: (be[b], 0, 0))],
            out_specs=pl.BlockSpec((MOE_BLOCK, D), lambda b, be: (b, 0)),
            scratch_shapes=[pltpu.VMEM((F, D), BF16)]),
        out_shape=jax.ShapeDtypeStruct((n_slots, D), F32),
        compiler_params=_params("arbitrary"),
    )(block_exp, a_all, P['w_down_e'])

    y_all = pl.pallas_call(
        _combine_kernel,
        grid_spec=pltpu.PrefetchScalarGridSpec(
            num_scalar_prefetch=2,
            grid=(R // MG,),
            in_specs=[pl.BlockSpec(memory_space=pl.ANY),
                      pl.BlockSpec((MG, D), lambda i, d0, d1: (i, 0)),
                      pl.BlockSpec((1, MG, D),
                                   lambda i, d0, d1: (jnp.where(i * MG < RP, (i * MG) // S, B), 0, 5)),
                      pl.BlockSpec((MG, ncol), lambda i, d0, d1: (i, 0))],
            out_specs=pl.BlockSpec((MG, D), lambda i, d0, d1: (i, 0)),
            scratch_shapes=[pltpu.VMEM((2, TOP_K, MG, D), F32), pltpu.SemaphoreType.DMA((2,))]),
        out_shape=jax.ShapeDtypeStruct((R, D), F32),
        compiler_params=_params("arbitrary"),
    )(dest[:, 0], dest[:, 1], y_slots, x1_all, mod_all, ew_all)

    return y_all, new_conv_p, new_conv_s, ckv_all, kpe_all


def kernel(x_prompt, x_sample, state_conv, cache_ckv, cache_kpe, page_table, c_prompt, c_sample, w_ada, b_ada, g_norm_mix, g_norm_ffn, w_in, w_conv, w_conv_out, g_q_lat, w_q_up, g_kv_lat, w_uk, w_uv, g_qk_q, g_qk_k, w_attn_out, w_o, w_route_group, b_route_group, w_route_expert, b_route_expert, w_gate_e, w_up_e, w_down_e):
    B, S, D = x_prompt.shape
    NB, T, _ = x_sample.shape
    depth = w_in.shape[0]
    RP = B * S
    KVR = cache_ckv.shape[-1]
    DR = cache_kpe.shape[-1]
    C = w_conv.shape[-1]
    x_all = jnp.concatenate([x_prompt.reshape(RP, D), x_sample.transpose(1, 0, 2).reshape(T * NB, D)], axis=0)
    c_rows = jnp.concatenate([jnp.repeat(c_prompt, NB, axis=0), c_sample], axis=0)
    weights = dict(w_ada=w_ada, b_ada=b_ada, g_norm_mix=g_norm_mix, g_norm_ffn=g_norm_ffn, w_in=w_in,
                   w_conv=w_conv, w_conv_out=w_conv_out, g_q_lat=g_q_lat, w_q_up=w_q_up, g_kv_lat=g_kv_lat,
                   w_uk=w_uk, w_uv=w_uv, g_qk_q=g_qk_q, g_qk_k=g_qk_k, w_attn_out=w_attn_out, w_o=w_o,
                   w_route_group=w_route_group, b_route_group=b_route_group, w_route_expert=w_route_expert,
                   b_route_expert=b_route_expert, w_gate_e=w_gate_e, w_up_e=w_up_e, w_down_e=w_down_e)
    conv_p, conv_s, ckv_p, kpe_p, ckv_s, kpe_s = [], [], [], [], [], []
    for l in range(depth):
        P = {k: v[l] for k, v in weights.items()}
        mod_all = _adaln(c_rows, P['w_ada'], P['b_ada']).reshape(B + 1, NB, 6 * D)
        x_all, cp, cs, ckv_all, kpe_all = _layer(x_all, (B, S, NB, T), mod_all, state_conv[l], cache_ckv,
                                                 cache_kpe, page_table, l, P)
        conv_p.append(cp)
        conv_s.append(cs)
        ckv_p.append(ckv_all[:RP].reshape(B, S, KVR))
        kpe_p.append(kpe_all[:RP].reshape(B, S, DR))
        ckv_s.append(ckv_all[RP:].reshape(T, NB, KVR).transpose(1, 0, 2))
        kpe_s.append(kpe_all[RP:].reshape(T, NB, DR).transpose(1, 0, 2))
    y_prompt = x_all[:RP].reshape(B, S, D)
    y_sample = x_all[RP:].reshape(T, NB, D).transpose(1, 0, 2)
    return (y_prompt, y_sample, jnp.stack(conv_p), jnp.stack(conv_s), jnp.stack(ckv_p), jnp.stack(kpe_p),
            jnp.stack(ckv_s), jnp.stack(kpe_s))
```

```python
import functools

import jax
import jax.numpy as jnp
import numpy as np
from jax import lax
from jax.experimental import pallas as pl
from jax.experimental.pallas import tpu as pltpu

EPS = 1e-6
ROPE_BASE = 10000.0
NEG_INF = -1e30
TOP_K = 2
BF16 = jnp.bfloat16
F32 = jnp.float32
VMEM_LIMIT_BYTES = 56 * 1024 * 1024
LANES = 128
SUBLANES = 8
ROW_TILE = 512
MOE_BLOCK = 128
PAGES_PER_STEP = 8


def _params(*sem):
    return pltpu.CompilerParams(dimension_semantics=sem, vmem_limit_bytes=VMEM_LIMIT_BYTES)


def _divisor_tile(n, pref, mult):
    best = None
    t = mult
    while t <= min(n, pref):
        if n % t == 0:
            best = t
        t += mult
    return best if best is not None else n


def _dot(a, b):
    return jnp.dot(a, b, preferred_element_type=F32)


def _dot_nt(a, b):
    return lax.dot_general(a, b, (((1,), (1,)), ((), ())), preferred_element_type=F32)


def _sigmoid(x):
    return 1.0 / (1.0 + jnp.exp(-x))


def _rms(x):
    return x * lax.rsqrt(jnp.mean(x * x, axis=-1, keepdims=True) + EPS)


def _expand_mod(m, rows):
    reps = rows // m.shape[0]
    return m if reps == 1 else jnp.concatenate([m] * reps, axis=0)


def _rope_rows(x, cos2, sin2):
    half = x.shape[1] // 2
    swapped = jnp.concatenate([x[:, half:], x[:, :half]], axis=1)
    return x * cos2 + swapped * sin2


def _ada_kernel(c_ref, w_ref, b_ref, o_ref):
    c = c_ref[...]
    s = (c * _sigmoid(c)).astype(BF16)
    o_ref[...] = _dot(s, w_ref[...].astype(BF16)) + b_ref[...]


def _adaln(c_rows, w_ada, b_ada):
    r, d = c_rows.shape
    n = w_ada.shape[1]
    tn = _divisor_tile(n, 1024, LANES)
    return pl.pallas_call(
        _ada_kernel,
        grid=(n // tn,),
        in_specs=[pl.BlockSpec((r, d), lambda j: (0, 0)),
                  pl.BlockSpec((d, tn), lambda j: (0, j)),
                  pl.BlockSpec((1, tn), lambda j: (0, j))],
        out_specs=pl.BlockSpec((r, tn), lambda j: (0, j)),
        out_shape=jax.ShapeDtypeStruct((r, n), F32),
        compiler_params=_params("parallel"),
    )(c_rows, w_ada, b_ada.reshape(1, n))


def _norm_kernel(x_ref, g_ref, sc_ref, sh_ref, o_ref):
    x = x_ref[...]
    rows = x.shape[0]
    y = _rms(x) * g_ref[...]
    y = y * (1.0 + _expand_mod(sc_ref[0], rows)) + _expand_mod(sh_ref[0], rows)
    o_ref[...] = y.astype(o_ref.dtype)


def _conv_kernel(h_ref, wh_ref, wb_ref, wc_ref, wconv_ref, hist_ref, z_ref, st_ref, carry_ref, *,
                 tiles_per_seq, n_prompt_tiles, nb):
    i = pl.program_id(0)
    j = pl.program_id(1)
    h = h_ref[...]
    u = _dot(h, wc_ref[...]) * _dot(h, wh_ref[...])
    bg = _dot(h, wb_ref[...])
    tm = u.shape[0]
    w = wconv_ref[...]

    @pl.when(i < n_prompt_tiles)
    def _():
        @pl.when(i % tiles_per_seq == 0)
        def _():
            carry_ref[j] = jnp.zeros(carry_ref.shape[1:], F32)

        tail = carry_ref[j]
        row8 = lax.broadcasted_iota(jnp.int32, (SUBLANES, 1), 0)

        def shifted(k):
            ur = pltpu.roll(u, k, 0)
            top = jnp.where(row8 < k, pltpu.roll(tail, k, 0), ur[:SUBLANES])
            return jnp.concatenate([top, ur[SUBLANES:]], axis=0)

        y = u * w[2:3] + shifted(2) * w[0:1] + shifted(1) * w[1:2]
        z_ref[...] = (bg * y).astype(z_ref.dtype)
        carry_ref[j] = u[tm - SUBLANES:]

    @pl.when(i >= n_prompt_tiles)
    def _():
        ext = jnp.concatenate([hist_ref[...], u], axis=0)
        y = ext[2 * nb:] * w[2:3] + ext[:tm] * w[0:1] + ext[nb:nb + tm] * w[1:2]
        z_ref[...] = (bg * y).astype(z_ref.dtype)

    st_ref[0] = u[tm - 2 * nb:]


def _q_kernel(h_ref, wqd_ref, gq_ref, wup_ref, gn_ref, gp_ref, cos_ref, sin_ref, qn_ref, qp_ref, *,
              n_heads, d_nope, d_rope, scale):
    qd = _dot(h_ref[...], wqd_ref[...])
    qdn = (_rms(qd) * gq_ref[...]).astype(BF16)
    qu = _dot(qdn, wup_ref[...])
    cos2 = cos_ref[...]
    sin2 = sin_ref[...]
    d_qk = d_nope + d_rope
    for hh in range(n_heads):
        qn = qu[:, hh * d_nope:(hh + 1) * d_nope]
        qp = qu[:, n_heads * d_nope + hh * d_rope:n_heads * d_nope + (hh + 1) * d_rope]
        ss = jnp.sum(qn * qn, axis=-1, keepdims=True) + jnp.sum(qp * qp, axis=-1, keepdims=True)
        r = lax.rsqrt(ss / d_qk + EPS)
        qn = qn * r * gn_ref[...]
        qp = _rope_rows(qp * r * gp_ref[...], cos2, sin2)
        qn_ref[:, hh * d_nope:(hh + 1) * d_nope] = (qn * scale).astype(qn_ref.dtype)
        qp_ref[hh] = (qp * scale).astype(qp_ref.dtype)


def _rope_cols(x, cos_t, sin_t):
    half = x.shape[0] // 2
    x1, x2 = x[:half], x[half:]
    return jnp.concatenate([x1 * cos_t - x2 * sin_t, x2 * cos_t + x1 * sin_t], axis=0)


def _kv_kernel(h_ref, wkvd_ref, wkpe_ref, wkpet_ref, gkv_ref, wukt_ref, wuv_ref, gn_ref, gp_ref, cos_ref, sin_ref,
               ckv_ref, kpe_ref, kt_ref, v_ref, *, n_heads, d_nope, d_rope):
    h = h_ref[...]
    ckv = _rms(_dot(h, wkvd_ref[...])) * gkv_ref[...]
    ckv_ref[...] = ckv
    kpe_ref[...] = _dot(h, wkpe_ref[...])
    cb = ckv.astype(BF16)
    v_ref[...] = _dot(cb, wuv_ref[...]).astype(v_ref.dtype)
    knt = _dot_nt(wukt_ref[...], cb)
    kpt = _dot_nt(wkpet_ref[...], h)
    sp = jnp.sum(kpt * kpt, axis=0, keepdims=True)
    krt = _rope_cols(kpt * gp_ref[...], cos_ref[...], sin_ref[...])
    d_qk = d_nope + d_rope
    for hh in range(n_heads):
        kn = knt[hh * d_nope:(hh + 1) * d_nope]
        r = lax.rsqrt((jnp.sum(kn * kn, axis=0, keepdims=True) + sp) / d_qk + EPS)
        kt_ref[hh, 0, :d_nope, :] = (kn * r * gn_ref[...]).astype(kt_ref.dtype)
        kt_ref[hh, 0, d_nope:, :] = (krt * r).astype(kt_ref.dtype)


def _flash_kernel(qn_ref, qp_ref, kt_ref, v_ref, o_ref, vext_ref, m_ref, acc_ref, *, tq):
    qi = pl.program_id(2)
    dv = v_ref.shape[1]

    @pl.when(qi == 0)
    def _():
        vext_ref[:, :dv] = v_ref[...]
        vext_ref[:, dv:] = jnp.ones((vext_ref.shape[0], vext_ref.shape[1] - dv), vext_ref.dtype)

    q = jnp.concatenate([qn_ref[...], qp_ref[0]], axis=1)
    m_ref[...] = jnp.full(m_ref.shape, NEG_INF, F32)
    acc_ref[...] = jnp.zeros(acc_ref.shape, F32)
    reps_s = tq // m_ref.shape[1]
    reps_a = acc_ref.shape[1] // m_ref.shape[1]

    def chunk(c, diagonal):
        s = _dot(q, kt_ref[0, c])
        if diagonal:
            row = lax.broadcasted_iota(jnp.int32, s.shape, 0)
            col = lax.broadcasted_iota(jnp.int32, s.shape, 1)
            s = jnp.where(col <= row, s, NEG_INF)
        m_prev = m_ref[...]
        m_new = jnp.maximum(m_prev, jnp.max(s, axis=-1, keepdims=True))
        a = jnp.exp(m_prev - m_new)
        p = jnp.exp(s - jnp.concatenate([m_new] * reps_s, axis=1))
        rows = pl.ds(pl.multiple_of(c * tq, tq), tq)
        acc_ref[...] = (jnp.concatenate([a] * reps_a, axis=1) * acc_ref[...]
                        + _dot(p.astype(BF16), vext_ref[rows, :]))
        m_ref[...] = m_new

    def body(c, carry):
        chunk(c, False)
        return carry

    lax.fori_loop(0, qi, body, 0)
    chunk(qi, True)
    acc = acc_ref[...]
    o_ref[...] = (acc[:, :dv] / acc[:, dv:2 * dv]).astype(o_ref.dtype)


def _qabs_kernel(qn_ref, gn_ref, wuk_ref, o_ref):
    q = (qn_ref[...].astype(F32) * gn_ref[...]).astype(BF16)
    o_ref[0] = _dot_nt(q, wuk_ref[...]).astype(o_ref.dtype)


def _paged_kernel(pt_ref, qt_ref, qpe_ref, *refs, pps, layer, n_heads, d_qk, t_new):
    (ckv_hbm, kpe_hbm, cos_ref, sin_ref, cn_ref, kn_ref, cosn_ref, sinn_ref, maskn_ref, wukt_ref, gp_ref,
     o_ref, m_ref, l_ref, acc_ref, cbp_ref, krtp_ref, rinvp_ref, ckv_buf, kpe_buf, sem) = refs
    p = pl.program_id(1)
    n_p = pl.num_programs(1)
    g = pl.program_id(0) * n_p + p
    slot = g % 2

    def page_copies(step, s):
        out = []
        for k in range(pps):
            pg = pt_ref[step * pps + k]
            out.append(pltpu.make_async_copy(ckv_hbm.at[layer, pg], ckv_buf.at[s, k], sem.at[s, 0]))
            out.append(pltpu.make_async_copy(kpe_hbm.at[layer, pg], kpe_buf.at[s, k], sem.at[s, 1]))
        return out

    @pl.when(g == 0)
    def _():
        for c in page_copies(0, 0):
            c.start()

    @pl.when(g + 1 < pl.num_programs(0) * n_p)
    def _():
        for c in page_copies(g + 1, 1 - slot):
            c.start()

    pltpu.make_async_copy(ckv_hbm.at[layer, pl.ds(0, pps)], ckv_buf.at[slot], sem.at[slot, 0]).wait()
    pltpu.make_async_copy(kpe_hbm.at[layer, pl.ds(0, pps)], kpe_buf.at[slot], sem.at[slot, 1]).wait()

    @pl.when(p == 0)
    def _():
        m_ref[...] = jnp.full(m_ref.shape, NEG_INF, F32)
        l_ref[...] = jnp.zeros(l_ref.shape, F32)
        acc_ref[...] = jnp.zeros(acc_ref.shape, F32)
        cbp_ref[...] = jnp.zeros(cbp_ref.shape, cbp_ref.dtype)
        krtp_ref[...] = jnp.zeros(krtp_ref.shape, krtp_ref.dtype)
        rinvp_ref[...] = jnp.zeros(rinvp_ref.shape, F32)

    qt = qt_ref[0]
    qpe = qpe_ref[0]
    d_nope = wukt_ref.shape[0] // n_heads

    def key_stats(cb, kpt, cos_t, sin_t):
        nk = cb.shape[0]
        kt = _dot_nt(wukt_ref[...], cb)
        ssn = jnp.sum((kt * kt).reshape(n_heads, d_nope, nk), axis=1)
        ssp = jnp.sum(kpt * kpt, axis=0, keepdims=True)
        rinv = lax.rsqrt((ssn + ssp) / d_qk + EPS)
        krt = _rope_cols(kpt * gp_ref[...], cos_t, sin_t).astype(BF16)
        return rinv, krt

    def scores(cb, krt, rinv):
        return (_dot_nt(qt, cb) + _dot(qpe, krt)) * jnp.concatenate([rinv] * t_new, axis=0)

    def accumulate(s, cb, weight=None):
        m_prev = m_ref[...]
        m_new = jnp.maximum(m_prev, jnp.max(s, axis=-1, keepdims=True))
        a = jnp.exp(m_prev - m_new)
        e = jnp.exp(s - m_new)
        if weight is not None:
            e = e * weight
        l_ref[...] = a * l_ref[...] + jnp.sum(e, axis=-1, keepdims=True)
        acc_ref[...] = a * acc_ref[...] + _dot(e.astype(BF16), cb)
        m_ref[...] = m_new

    cur = p % 2
    prev = 1 - cur
    page, kvr = ckv_buf.shape[2], ckv_buf.shape[3]
    cbp_ref[cur] = ckv_buf[slot].reshape(pps * page, kvr).astype(BF16)
    cb = cbp_ref[cur]
    cb_prev = cbp_ref[prev]
    s_prev = scores(cb_prev, krtp_ref[prev], rinvp_ref[prev])
    kpt = jnp.concatenate([kpe_buf[slot, k] for k in range(pps)], axis=1)
    rinv, krt = key_stats(cb, kpt, cos_ref[...], sin_ref[...])
    accumulate(jnp.where(p > 0, s_prev, NEG_INF), cb_prev, jnp.where(p > 0, 1.0, 0.0))
    krtp_ref[cur] = krt
    rinvp_ref[cur] = rinv

    @pl.when(p == pl.num_programs(1) - 1)
    def _():
        accumulate(scores(cb, krt, rinv), cb)
        pad = maskn_ref.shape[1] - cn_ref.shape[1]
        cb_new = jnp.concatenate([cn_ref[0], jnp.zeros((pad, cn_ref.shape[2]), F32)], axis=0).astype(BF16)
        rinv_new, krt_new = key_stats(cb_new, kn_ref[0], cosn_ref[...], sinn_ref[...])
        s_new = jnp.where(maskn_ref[...] > 0.0, scores(cb_new, krt_new, rinv_new), NEG_INF)
        accumulate(s_new, cb_new)
        o_ref[0] = (acc_ref[...] / l_ref[...]).astype(o_ref.dtype)


def _ouv_kernel(ol_ref, wuv_ref, o_ref):
    o_ref[...] = _dot(ol_ref[0], wuv_ref[...]).astype(o_ref.dtype)


def _mix_kernel(h_ref, z_ref, op_ref, os_ref, wga_ref, wgb_ref, wco_ref, wao_ref, m_ref, *, n_prompt_tiles):
    h = h_ref[...]
    ga = _dot(h, wga_ref[...])
    gb = _dot(h, wgb_ref[...])
    y_conv = _dot(z_ref[...], wco_ref[...])
    o = jnp.where(pl.program_id(0) < n_prompt_tiles, op_ref[...], os_ref[...])
    y_attn = _dot(o, wao_ref[...])
    m_ref[...] = (_sigmoid(ga) * y_conv + _sigmoid(gb) * y_attn).astype(m_ref.dtype)


def _wo_kernel(m_ref, wo_ref, x_ref, gt_ref, g_ref, sc_ref, sh_ref, wr_ref, br_ref, gid_ref,
               x1_ref, h2_ref, eid_ref, ew_ref, cnt_ref, run_ref, *, n_groups):
    rows = x_ref.shape[0]

    @pl.when(pl.program_id(0) == 0)
    def _():
        run_ref[...] = jnp.zeros(run_ref.shape, F32)

    x1 = x_ref[...] + _expand_mod(gt_ref[0], rows) * _dot(m_ref[...], wo_ref[...])
    x1_ref[...] = x1
    h2 = _rms(x1) * g_ref[...]
    h2 = h2 * (1.0 + _expand_mod(sc_ref[0], rows)) + _expand_mod(sh_ref[0], rows)
    h2_ref[...] = h2
    logits = _dot(h2.astype(BF16), wr_ref[...]) + br_ref[...]
    icol = lax.broadcasted_iota(jnp.int32, logits.shape, 1)
    col = icol.astype(F32)
    gid = gid_ref[...]
    big = float(logits.shape[1])
    lg = jnp.where(col < n_groups, logits, NEG_INF)
    mg = jnp.max(lg, axis=-1, keepdims=True)
    g_star = jnp.min(jnp.where(lg == mg, col, big), axis=-1, keepdims=True)
    p_top = 1.0 / jnp.sum(jnp.exp(lg - mg), axis=-1, keepdims=True)
    le = jnp.where(gid == g_star, logits, NEG_INF)
    v1 = jnp.max(le, axis=-1, keepdims=True)
    i1 = jnp.min(jnp.where(le == v1, col, big), axis=-1, keepdims=True)
    le2 = jnp.where(col == i1, NEG_INF, le)
    v2 = jnp.max(le2, axis=-1, keepdims=True)
    i2 = jnp.min(jnp.where(le2 == v2, col, big), axis=-1, keepdims=True)
    e2 = jnp.exp(v2 - v1)
    w1 = p_top * (1.0 / (1.0 + e2))
    w2 = p_top * (e2 / (1.0 + e2))
    oh1 = col == i1
    oh2 = col == i2
    onehot = jnp.where(jnp.logical_or(oh1, oh2), 1.0, 0.0)
    r_io = lax.broadcasted_iota(jnp.int32, (rows, rows), 0)
    c_io = lax.broadcasted_iota(jnp.int32, (rows, rows), 1)
    tri = jnp.where(c_io < r_io, 1.0, 0.0).astype(BF16)
    before = _dot(tri, onehot.astype(BF16)) + run_ref[...]
    rank1 = jnp.sum(jnp.where(oh1, before, 0.0), axis=-1, keepdims=True)
    rank2 = jnp.sum(jnp.where(oh2, before, 0.0), axis=-1, keepdims=True)
    run_ref[...] = run_ref[...] + jnp.sum(onehot, axis=0, keepdims=True)
    cnt_ref[...] = jnp.broadcast_to(run_ref[...], cnt_ref.shape)
    meta = jnp.where(icol == 0, i1 - n_groups, jnp.where(icol == 1, i2 - n_groups,
                     jnp.where(icol == 2, rank1, jnp.where(icol == 3, rank2, 0.0))))
    eid_ref[...] = meta.astype(jnp.int32)
    ew_ref[...] = jnp.where(icol == 0, w1, jnp.where(icol == 1, w2, 0.0))


def _row_gather(idx_ref, base, n, src_hbm, dst, sem):
    def body(r, carry):
        tok = idx_ref[base + r]
        pltpu.make_async_copy(src_hbm.at[pl.ds(tok, 1), :], dst.at[pl.ds(r, 1), :], sem).start()
        return carry
    lax.fori_loop(0, n, body, 0, unroll=8)


def _rows_wait(n, src_hbm, dst, sem):
    pltpu.make_async_copy(src_hbm.at[pl.ds(0, n), :], dst, sem).wait()


def _expert_changed(be_ref, b):
    return jnp.logical_or(b == 0, be_ref[b] != be_ref[jnp.maximum(b - 1, 0)])


def _moe_up_kernel(be_ref, tok_ref, h_hbm, wg_ref, wu_ref, a_ref, xbuf, wg_bf, wu_bf, sem):
    b = pl.program_id(0)
    nb = pl.num_programs(0)
    blk = xbuf.shape[1]
    slot = b % 2

    @pl.when(_expert_changed(be_ref, b))
    def _():
        wg_bf[...] = wg_ref[0].astype(BF16)
        wu_bf[...] = wu_ref[0].astype(BF16)

    @pl.when(b == 0)
    def _():
        _row_gather(tok_ref, 0, blk, h_hbm, xbuf.at[0], sem.at[0])

    @pl.when(b + 1 < nb)
    def _():
        _row_gather(tok_ref, (b + 1) * blk, blk, h_hbm, xbuf.at[1 - slot], sem.at[1 - slot])

    _rows_wait(blk, h_hbm, xbuf.at[slot], sem.at[slot])
    x = xbuf[slot].astype(BF16)
    g = _dot(x, wg_bf[...])
    u = _dot(x, wu_bf[...])
    a_ref[...] = (g * _sigmoid(g) * u).astype(a_ref.dtype)


def _moe_down_kernel(be_ref, a_ref, wd_ref, y_ref, wd_bf):
    @pl.when(_expert_changed(be_ref, pl.program_id(0)))
    def _():
        wd_bf[...] = wd_ref[0].astype(BF16)

    y_ref[...] = _dot(a_ref[...], wd_bf[...])


def _combine_kernel(d0_ref, d1_ref, y_hbm, x1_ref, gt_ref, ew_ref, o_ref, ybuf, sem):
    i = pl.program_id(0)
    n = pl.num_programs(0)
    rows = x1_ref.shape[0]
    slot = i % 2

    def start(step, s):
        _row_gather(d0_ref, step * rows, rows, y_hbm, ybuf.at[s, 0], sem.at[s])
        _row_gather(d1_ref, step * rows, rows, y_hbm, ybuf.at[s, 1], sem.at[s])

    @pl.when(i == 0)
    def _():
        start(0, 0)

    @pl.when(i + 1 < n)
    def _():
        start(i + 1, 1 - slot)

    _rows_wait(rows, y_hbm, ybuf.at[slot, 0], sem.at[slot])
    _rows_wait(rows, y_hbm, ybuf.at[slot, 1], sem.at[slot])
    ew = ew_ref[...]
    y = ybuf[slot, 0] * ew[:, 0:1] + ybuf[slot, 1] * ew[:, 1:2]
    o_ref[...] = x1_ref[...] + _expand_mod(gt_ref[0], rows) * y


def _rope_tables(pos, d_rope):
    half = d_rope // 2
    inv = ROPE_BASE ** (-jnp.arange(half, dtype=F32) / half)
    ang = pos.astype(F32)[:, None] * inv[None, :]
    cos, sin = jnp.cos(ang), jnp.sin(ang)
    return jnp.concatenate([cos, cos], axis=1), jnp.concatenate([-sin, sin], axis=1)


def _layer(x_all, dims, mod_all, state_conv_l, cache_ckv, cache_kpe, page_table, l, P):
    (B, S, NB, T) = dims
    R, D = x_all.shape
    RP, RS = B * S, T * NB
    C = P['w_conv'].shape[1]
    QR = P['g_q_lat'].shape[0]
    KVR = P['g_kv_lat'].shape[0]
    H, DN = P['w_uk'].shape[1], P['w_uk'].shape[2]
    DV = P['w_uv'].shape[2]
    DR = cache_kpe.shape[-1]
    DQK = DN + DR
    scale = DQK ** -0.5
    PAGE = cache_ckv.shape[2]
    NPAGES = page_table.shape[1]
    PAST = NPAGES * PAGE
    G = P['w_route_group'].shape[1]
    E = P['w_route_expert'].shape[1]
    F = P['w_gate_e'].shape[2]
    MG = NB

    tm = _divisor_tile(int(np.gcd(S, RS)), ROW_TILE, MG)
    n_tiles, np_tiles = R // tm, RP // tm

    def mod_spec(chunk):
        def idx(i, *_):
            return (jnp.where(i < np_tiles, (i * tm) // S, B), 0, chunk)
        return pl.BlockSpec((1, MG, D), idx)

    def row_spec(width, tile=tm):
        return pl.BlockSpec((tile, width), lambda i, *_: (i, 0))

    def full_spec(shape):
        nd = len(shape)
        return pl.BlockSpec(shape, lambda *_: (0,) * nd)

    offs = np.cumsum([0, C, C, C, QR, KVR, DR, D, D])
    w_h, w_b, w_c, w_qd, w_kvd, w_kpe, w_ga, w_gb = [
        P['w_in'][:, offs[k]:offs[k + 1]].astype(BF16) for k in range(8)]
    w_up = P['w_q_up'].reshape(QR, H, DQK)
    w_up = jnp.concatenate([w_up[:, :, :DN].reshape(QR, H * DN), w_up[:, :, DN:].reshape(QR, H * DR)],
                           axis=1).astype(BF16)
    w_uk = P['w_uk'].reshape(KVR, H * DN).astype(BF16)
    w_uv = P['w_uv'].reshape(KVR, H * DV).astype(BF16)
    w_co = P['w_conv_out'].astype(BF16)
    w_ao = P['w_attn_out'].astype(BF16)
    w_o = P['w_o'].astype(BF16)
    gq_n, gq_p = P['g_qk_q'][:DN].reshape(1, DN), P['g_qk_q'][DN:].reshape(1, DR)
    gk_n, gk_p = P['g_qk_k'][:DN].reshape(1, DN), P['g_qk_k'][DN:].reshape(1, DR)

    pos_rows = jnp.concatenate([jnp.tile(jnp.arange(S), B), PAST + jnp.repeat(jnp.arange(T), NB)])
    cos_rows, sin_rows = _rope_tables(pos_rows, DR)

    h1 = pl.pallas_call(
        _norm_kernel,
        grid=(n_tiles,),
        in_specs=[row_spec(D), full_spec((1, D)), mod_spec(1), mod_spec(0)],
        out_specs=row_spec(D),
        out_shape=jax.ShapeDtypeStruct((R, D), BF16),
        compiler_params=_params("parallel"),
    )(x_all, P['g_norm_mix'].reshape(1, D), mod_all, mod_all)

    assert T >= 2 and S % RS == 0 and RS >= 2 * NB
    tc = _divisor_tile(C, 512, LANES)
    n_c = C // tc
    tiles_per_seq = S // RS
    n_ct = R // RS
    hist_t = state_conv_l.transpose(1, 0, 2).reshape(2 * NB, C)
    z_all, st_all = pl.pallas_call(
        functools.partial(_conv_kernel, tiles_per_seq=tiles_per_seq, n_prompt_tiles=RP // RS, nb=NB),
        grid=(n_ct, n_c),
        in_specs=[pl.BlockSpec((RS, D), lambda i, j: (i, 0)),
                  pl.BlockSpec((D, tc), lambda i, j: (0, j)),
                  pl.BlockSpec((D, tc), lambda i, j: (0, j)),
                  pl.BlockSpec((D, tc), lambda i, j: (0, j)),
                  pl.BlockSpec((3, tc), lambda i, j: (0, j)),
                  pl.BlockSpec((2 * NB, tc), lambda i, j: (0, j))],
        out_specs=[pl.BlockSpec((RS, tc), lambda i, j: (i, j)),
                   pl.BlockSpec((1, 2 * NB, tc), lambda i, j: (i, 0, j))],
        out_shape=[jax.ShapeDtypeStruct((R, C), BF16), jax.ShapeDtypeStruct((n_ct, 2 * NB, C), F32)],
        scratch_shapes=[pltpu.VMEM((n_c, SUBLANES, tc), F32)],
        compiler_params=_params("arbitrary", "arbitrary"),
    )(h1, w_h, w_b, w_c, P['w_conv'], hist_t)
    new_conv_p = st_all[tiles_per_seq - 1:RP // RS:tiles_per_seq, 2 * NB - 2:]
    new_conv_s = st_all[n_ct - 1].reshape(2, NB, C).transpose(1, 0, 2)
    sblk = RP // RS

    qn_all, qp_all = pl.pallas_call(
        functools.partial(_q_kernel, n_heads=H, d_nope=DN, d_rope=DR, scale=scale),
        grid=(n_tiles,),
        in_specs=[row_spec(D), full_spec((D, QR)), full_spec((1, QR)), full_spec((QR, H * DQK)),
                  full_spec((1, DN)), full_spec((1, DR)), row_spec(DR), row_spec(DR)],
        out_specs=[row_spec(H * DN), pl.BlockSpec((H, tm, DR), lambda i: (0, i, 0))],
        out_shape=[jax.ShapeDtypeStruct((R, H * DN), BF16), jax.ShapeDtypeStruct((H, R, DR), BF16)],
        compiler_params=_params("parallel"),
    )(h1, w_qd, P['g_q_lat'].reshape(1, QR), w_up, gq_n, gq_p, cos_rows, sin_rows)

    half = DR // 2
    w_ukt = w_uk.T
    gk_nc, gk_pc = gk_n.reshape(DN, 1), gk_p.reshape(DR, 1)
    cos_t = cos_rows[:, :half].T
    sin_t = sin_rows[:, half:].T
    ckv_all, kpe_all, kt_all, v_all = pl.pallas_call(
        functools.partial(_kv_kernel, n_heads=H, d_nope=DN, d_rope=DR),
        grid=(n_tiles,),
        in_specs=[row_spec(D), full_spec((D, KVR)), full_spec((D, DR)), full_spec((DR, D)), full_spec((1, KVR)),
                  full_spec((H * DN, KVR)), full_spec((KVR, H * DV)), full_spec((DN, 1)), full_spec((DR, 1)),
                  pl.BlockSpec((half, tm), lambda i: (0, i)), pl.BlockSpec((half, tm), lambda i: (0, i))],
        out_specs=[row_spec(KVR), row_spec(DR),
                   pl.BlockSpec((H, 1, DQK, tm), lambda i: (0, i, 0, 0)), row_spec(H * DV)],
        out_shape=[jax.ShapeDtypeStruct((R, KVR), F32), jax.ShapeDtypeStruct((R, DR), F32),
                   jax.ShapeDtypeStruct((H, n_tiles, DQK, tm), BF16),
                   jax.ShapeDtypeStruct((R, H * DV), BF16)],
        compiler_params=_params("parallel"),
    )(h1, w_kvd, w_kpe, w_kpe.T, P['g_kv_lat'].reshape(1, KVR), w_ukt, w_uv, gk_nc, gk_pc, cos_t, sin_t)

    tq = tm
    nq = S // tq
    o_p = pl.pallas_call(
        functools.partial(_flash_kernel, tq=tq),
        grid=(B, H, nq),
        in_specs=[pl.BlockSpec((tq, DN), lambda b, h, q: (b * nq + q, h)),
                  pl.BlockSpec((1, tq, DR), lambda b, h, q: (h, b * nq + q, 0)),
                  pl.BlockSpec((1, nq, DQK, tq), lambda b, h, q: (h, b, 0, 0)),
                  pl.BlockSpec((S, DV), lambda b, h, q: (b, h))],
        out_specs=pl.BlockSpec((tq, DV), lambda b, h, q: (b * nq + q, h)),
        out_shape=jax.ShapeDtypeStruct((RP, H * DV), BF16),
        scratch_shapes=[pltpu.VMEM((S, 2 * DV), BF16), pltpu.VMEM((tq, LANES), F32),
                        pltpu.VMEM((tq, 2 * DV), F32)],
        compiler_params=_params("parallel", "parallel", "arbitrary"),
    )(qn_all, qp_all, kt_all, v_all)

    qt = pl.pallas_call(
        _qabs_kernel,
        grid=(H,),
        in_specs=[pl.BlockSpec((RS, DN), lambda h: (sblk, h)), full_spec((1, DN)),
                  pl.BlockSpec((KVR, DN), lambda h: (0, h))],
        out_specs=pl.BlockSpec((1, RS, KVR), lambda h: (h, 0, 0)),
        out_shape=jax.ShapeDtypeStruct((H, RS, KVR), BF16),
        compiler_params=_params("parallel"),
    )(qn_all, gk_n, w_uk)
    QROWS = T * H
    qt = qt.reshape(H, T, NB, KVR).transpose(2, 1, 0, 3).reshape(NB, QROWS, KVR)
    qpe = qp_all[:, RP:].reshape(H, T, NB, DR).transpose(2, 1, 0, 3).reshape(NB, QROWS, DR)
    pos_keys = jnp.arange(PAST + PAGE)
    cos_k, sin_k = _rope_tables(pos_keys, DR)
    cos_kt, sin_kt = cos_k[:, :half].T, sin_k[:, half:].T
    t_pad = -(-T // SUBLANES) * SUBLANES
    ckv_s = ckv_all[RP:].reshape(T, NB, KVR).transpose(1, 0, 2)
    kpe_s = kpe_all[RP:].reshape(T, NB, DR).transpose(1, 0, 2)
    cn = jnp.pad(ckv_s, ((0, 0), (0, t_pad - T), (0, 0)))
    knt = jnp.pad(kpe_s.transpose(0, 2, 1), ((0, 0), (0, 0), (0, PAGE - T)))
    key_j = jnp.arange(PAGE)[None, :]
    row_t = (jnp.arange(QROWS) // H)[:, None]
    mask_new = ((key_j <= row_t) & (key_j < T)).astype(F32)
    cache_kpe_t = jnp.swapaxes(cache_kpe, 2, 3)
    pps = _divisor_tile(NPAGES, PAGES_PER_STEP, 1)
    n_steps = NPAGES // pps

    def cfull(shape):
        nd = len(shape)
        return pl.BlockSpec(shape, lambda b, p, pt: (0,) * nd)

    o_lat = pl.pallas_call(
        functools.partial(_paged_kernel, pps=pps, layer=l, n_heads=H, d_qk=DQK, t_new=T),
        grid_spec=pltpu.PrefetchScalarGridSpec(
            num_scalar_prefetch=1,
            grid=(NB, n_steps),
            in_specs=([pl.BlockSpec((1, QROWS, KVR), lambda b, p, pt: (b, 0, 0)),
                       pl.BlockSpec((1, QROWS, DR), lambda b, p, pt: (b, 0, 0)),
                       pl.BlockSpec(memory_space=pl.ANY), pl.BlockSpec(memory_space=pl.ANY)]
                      + [pl.BlockSpec((half, pps * PAGE), lambda b, p, pt: (0, p)),
                         pl.BlockSpec((half, pps * PAGE), lambda b, p, pt: (0, p)),
                         pl.BlockSpec((1, t_pad, KVR), lambda b, p, pt: (b, 0, 0)),
                         pl.BlockSpec((1, DR, PAGE), lambda b, p, pt: (b, 0, 0)),
                         pl.BlockSpec((half, PAGE), lambda b, p, pt: (0, NPAGES)),
                         pl.BlockSpec((half, PAGE), lambda b, p, pt: (0, NPAGES)),
                         cfull((QROWS, PAGE)), cfull((H * DN, KVR)), cfull((DR, 1))]),
            out_specs=pl.BlockSpec((1, QROWS, KVR), lambda b, p, pt: (b, 0, 0)),
            scratch_shapes=[pltpu.VMEM((QROWS, 1), F32), pltpu.VMEM((QROWS, 1), F32),
                            pltpu.VMEM((QROWS, KVR), F32), pltpu.VMEM((2, pps * PAGE, KVR), BF16),
                            pltpu.VMEM((2, DR, pps * PAGE), BF16), pltpu.VMEM((2, H, pps * PAGE), F32),
                            pltpu.VMEM((2, pps, PAGE, KVR), F32), pltpu.VMEM((2, pps, DR, PAGE), F32),
                            pltpu.SemaphoreType.DMA((2, 2))]),
        out_shape=jax.ShapeDtypeStruct((NB, QROWS, KVR), BF16),
        compiler_params=_params("arbitrary", "arbitrary"),
    )(page_table.reshape(NB * NPAGES), qt, qpe, cache_ckv, cache_kpe_t, cos_kt, sin_kt,
      cn, knt, cos_kt, sin_kt, mask_new, w_ukt, gk_pc)
    o_lat = o_lat.reshape(NB, T, H, KVR).transpose(2, 1, 0, 3).reshape(H, RS, KVR)
    o_s = pl.pallas_call(
        _ouv_kernel,
        grid=(H,),
        in_specs=[pl.BlockSpec((1, RS, KVR), lambda h: (h, 0, 0)),
                  pl.BlockSpec((KVR, DV), lambda h: (0, h))],
        out_specs=pl.BlockSpec((RS, DV), lambda h: (0, h)),
        out_shape=jax.ShapeDtypeStruct((RS, H * DV), BF16),
        compiler_params=_params("parallel"),
    )(o_lat, w_uv)

    tn = _divisor_tile(D, 512, LANES)
    tmm = _divisor_tile(int(np.gcd(RP, RS)), 1024, SUBLANES)
    npm = RP // tmm
    m_all = pl.pallas_call(
        functools.partial(_mix_kernel, n_prompt_tiles=npm),
        grid=(R // tmm, D // tn),
        in_specs=[pl.BlockSpec((tmm, D), lambda i, j: (i, 0)),
                  pl.BlockSpec((tmm, C), lambda i, j: (i, 0)),
                  pl.BlockSpec((tmm, H * DV), lambda i, j: (jnp.minimum(i, npm - 1), 0)),
                  pl.BlockSpec((tmm, H * DV), lambda i, j: (jnp.maximum(i - npm, 0), 0)),
                  pl.BlockSpec((D, tn), lambda i, j: (0, j)),
                  pl.BlockSpec((D, tn), lambda i, j: (0, j)),
                  pl.BlockSpec((C, tn), lambda i, j: (0, j)),
                  pl.BlockSpec((H * DV, tn), lambda i, j: (0, j))],
        out_specs=pl.BlockSpec((tmm, tn), lambda i, j: (i, j)),
        out_shape=jax.ShapeDtypeStruct((R, D), BF16),
        compiler_params=_params("parallel", "parallel"),
    )(h1, z_all, o_p, o_s, w_ga, w_gb, w_co, w_ao)

    ncol = -(-(G + E) // LANES) * LANES
    w_r = jnp.pad(jnp.concatenate([P['w_route_group'], P['w_route_expert']], axis=1),
                  ((0, 0), (0, ncol - G - E))).astype(BF16)
    b_r = jnp.pad(jnp.concatenate([P['b_route_group'], P['b_route_expert']]), (0, ncol - G - E)).reshape(1, ncol)
    colv = np.arange(ncol)
    gid = np.where((colv >= G) & (colv < G + E), (colv - G) // (E // G), -1).astype(np.float32).reshape(1, ncol)
    x1_all, h2_all, meta_all, ew_all, cnt = pl.pallas_call(
        functools.partial(_wo_kernel, n_groups=G),
        grid=(n_tiles,),
        in_specs=[row_spec(D), full_spec((D, D)), row_spec(D), mod_spec(2), full_spec((1, D)),
                  mod_spec(4), mod_spec(3), full_spec((D, ncol)), full_spec((1, ncol)), full_spec((1, ncol))],
        out_specs=[row_spec(D), row_spec(D), row_spec(ncol), row_spec(ncol), full_spec((SUBLANES, ncol))],
        out_shape=[jax.ShapeDtypeStruct((R, D), F32), jax.ShapeDtypeStruct((R, D), F32),
                   jax.ShapeDtypeStruct((R, ncol), jnp.int32), jax.ShapeDtypeStruct((R, ncol), F32),
                   jax.ShapeDtypeStruct((SUBLANES, ncol), F32)],
        scratch_shapes=[pltpu.VMEM((1, ncol), F32)],
        compiler_params=_params("arbitrary"),
    )(m_all, w_o, x_all, mod_all, P['g_norm_ffn'].reshape(1, D), mod_all, mod_all, w_r, b_r, jnp.asarray(gid))

    nk = R * TOP_K
    eid = meta_all[:, :TOP_K]
    rank = meta_all[:, TOP_K:2 * TOP_K]
    counts = cnt[0, G:G + E].astype(jnp.int32)
    padded = (counts + MOE_BLOCK - 1) // MOE_BLOCK * MOE_BLOCK
    pad_end = jnp.cumsum(padded)
    pad_start = pad_end - padded
    dest = (pad_start[eid] + rank).astype(jnp.int32)
    n_blocks = (nk + E * (MOE_BLOCK - 1) + MOE_BLOCK - 1) // MOE_BLOCK
    n_slots = n_blocks * MOE_BLOCK
    flat_t = jnp.repeat(jnp.arange(R, dtype=jnp.int32), TOP_K)
    slot_tok = jnp.zeros((n_slots,), jnp.int32).at[dest.reshape(nk)].set(flat_t, unique_indices=True)
    block_start = jnp.arange(n_blocks, dtype=jnp.int32) * MOE_BLOCK
    block_exp = jnp.minimum(jnp.sum(pad_end[None, :] <= block_start[:, None], axis=1), E - 1).astype(jnp.int32)

    a_all = pl.pallas_call(
        _moe_up_kernel,
        grid_spec=pltpu.PrefetchScalarGridSpec(
            num_scalar_prefetch=2,
            grid=(n_blocks,),
            in_specs=[pl.BlockSpec(memory_space=pl.ANY),
                      pl.BlockSpec((1, D, F), lambda b, be, tok: (be[b], 0, 0)),
                      pl.BlockSpec((1, D, F), lambda b, be, tok: (be[b], 0, 0))],
            out_specs=pl.BlockSpec((MOE_BLOCK, F), lambda b, be, tok: (b, 0)),
            scratch_shapes=[pltpu.VMEM((2, MOE_BLOCK, D), F32), pltpu.VMEM((D, F), BF16),
                            pltpu.VMEM((D, F), BF16), pltpu.SemaphoreType.DMA((2,))]),
        out_shape=jax.ShapeDtypeStruct((n_slots, F), BF16),
        compiler_params=_params("arbitrary"),
    )(block_exp, slot_tok, h2_all, P['w_gate_e'], P['w_up_e'])

    y_slots = pl.pallas_call(
        _moe_down_kernel,
        grid_spec=pltpu.PrefetchScalarGridSpec(
            num_scalar_prefetch=1,
            grid=(n_blocks,),
            in_specs=[pl.BlockSpec((MOE_BLOCK, F), lambda b, be: (b, 0)),
                      pl.BlockSpec((1, F, D), lambda b, be: (be[b], 0, 0))],
            out_specs=pl.BlockSpec((MOE_BLOCK, D), lambda b, be: (b, 0)),
            scratch_shapes=[pltpu.VMEM((F, D), BF16)]),
        out_shape=jax.ShapeDtypeStruct((n_slots, D), F32),
        compiler_params=_params("arbitrary"),
    )(block_exp, a_all, P['w_down_e'])

    y_all = pl.pallas_call(
        _combine_kernel,
        grid_spec=pltpu.PrefetchScalarGridSpec(
            num_scalar_prefetch=2,
            grid=(R // MG,),
            in_specs=[pl.BlockSpec(memory_space=pl.ANY),
                      pl.BlockSpec((MG, D), lambda i, d0, d1: (i, 0)),
                      pl.BlockSpec((1, MG, D),
                                   lambda i, d0, d1: (jnp.where(i * MG < RP, (i * MG) // S, B), 0, 5)),
                      pl.BlockSpec((MG, ncol), lambda i, d0, d1: (i, 0))],
            out_specs=pl.BlockSpec((MG, D), lambda i, d0, d1: (i, 0)),
            scratch_shapes=[pltpu.VMEM((2, TOP_K, MG, D), F32), pltpu.SemaphoreType.DMA((2,))]),
        out_shape=jax.ShapeDtypeStruct((R, D), F32),
        compiler_params=_params("arbitrary"),
    )(dest[:, 0], dest[:, 1], y_slots, x1_all, mod_all, ew_all)

    return y_all, new_conv_p, new_conv_s, ckv_all, kpe_all


def kernel(x_prompt, x_sample, state_conv, cache_ckv, cache_kpe, page_table, c_prompt, c_sample, w_ada, b_ada, g_norm_mix, g_norm_ffn, w_in, w_conv, w_conv_out, g_q_lat, w_q_up, g_kv_lat, w_uk, w_uv, g_qk_q, g_qk_k, w_attn_out, w_o, w_route_group, b_route_group, w_route_expert, b_route_expert, w_gate_e, w_up_e, w_down_e):
    B, S, D = x_prompt.shape
    NB, T, _ = x_sample.shape
    depth = w_in.shape[0]
    RP = B * S
    KVR = cache_ckv.shape[-1]
    DR = cache_kpe.shape[-1]
    C = w_conv.shape[-1]
    x_all = jnp.concatenate([x_prompt.reshape(RP, D), x_sample.transpose(1, 0, 2).reshape(T * NB, D)], axis=0)
    c_rows = jnp.concatenate([jnp.repeat(c_prompt, NB, axis=0), c_sample], axis=0)
    weights = dict(w_ada=w_ada, b_ada=b_ada, g_norm_mix=g_norm_mix, g_norm_ffn=g_norm_ffn, w_in=w_in,
                   w_conv=w_conv, w_conv_out=w_conv_out, g_q_lat=g_q_lat, w_q_up=w_q_up, g_kv_lat=g_kv_lat,
                   w_uk=w_uk, w_uv=w_uv, g_qk_q=g_qk_q, g_qk_k=g_qk_k, w_attn_out=w_attn_out, w_o=w_o,
                   w_route_group=w_route_group, b_route_group=b_route_group, w_route_expert=w_route_expert,
                   b_route_expert=b_route_expert, w_gate_e=w_gate_e, w_up_e=w_up_e, w_down_e=w_down_e)
    conv_p, conv_s, ckv_p, kpe_p, ckv_s, kpe_s = [], [], [], [], [], []
    for l in range(depth):
        P = {k: v[l] for k, v in weights.items()}
        mod_all = _adaln(c_rows, P['w_ada'], P['b_ada']).reshape(B + 1, NB, 6 * D)
        x_all, cp, cs, ckv_all, kpe_all = _layer(x_all, (B, S, NB, T), mod_all, state_conv[l], cache_ckv,
                                                 cache_kpe, page_table, l, P)
        conv_p.append(cp)
        conv_s.append(cs)
        ckv_p.append(ckv_all[:RP].reshape(B, S, KVR))
        kpe_p.append(kpe_all[:RP].reshape(B, S, DR))
        ckv_s.append(ckv_all[RP:].reshape(T, NB, KVR).transpose(1, 0, 2))
        kpe_s.append(kpe_all[RP:].reshape(T, NB, DR).transpose(1, 0, 2))
    y_prompt = x_all[:RP].reshape(B, S, D)
    y_sample = x_all[RP:].reshape(T, NB, D).transpose(1, 0, 2)
    return (y_prompt, y_sample, jnp.stack(conv_p), jnp.stack(conv_s), jnp.stack(ckv_p), jnp.stack(kpe_p),
            jnp.stack(ckv_s), jnp.stack(kpe_s))
```

```python
import functools

import jax
import jax.numpy as jnp
import numpy as np
from jax import lax
from jax.experimental import pallas as pl
from jax.experimental.pallas import tpu as pltpu

EPS = 1e-6
ROPE_BASE = 10000.0
NEG_INF = -1e30
LOG2_E = 1.4426950408889634
TOP_K = 2
BF16 = jnp.bfloat16
F32 = jnp.float32
VMEM_LIMIT_BYTES = 56 * 1024 * 1024
LANES = 128
SUBLANES = 8
ROW_TILE = 512
MOE_BLOCK = 128
PAGES_PER_STEP = 8


def _params(*sem):
    return pltpu.CompilerParams(dimension_semantics=sem, vmem_limit_bytes=VMEM_LIMIT_BYTES)


def _divisor_tile(n, pref, mult):
    best = None
    t = mult
    while t <= min(n, pref):
        if n % t == 0:
            best = t
        t += mult
    return best if best is not None else n


def _dot(a, b):
    return jnp.dot(a, b, preferred_element_type=F32)


def _dot_nt(a, b):
    return lax.dot_general(a, b, (((1,), (1,)), ((), ())), preferred_element_type=F32)


def _sigmoid(x):
    return 1.0 / (1.0 + jnp.exp(-x))


def _rms(x):
    return x * lax.rsqrt(jnp.mean(x * x, axis=-1, keepdims=True) + EPS)


def _expand_mod(m, rows):
    reps = rows // m.shape[0]
    return m if reps == 1 else jnp.concatenate([m] * reps, axis=0)


def _rope_rows(x, cos2, sin2):
    half = x.shape[1] // 2
    swapped = jnp.concatenate([x[:, half:], x[:, :half]], axis=1)
    return x * cos2 + swapped * sin2


def _ada_kernel(c_ref, w_ref, b_ref, o_ref):
    c = c_ref[...]
    s = (c * _sigmoid(c)).astype(BF16)
    o_ref[...] = _dot(s, w_ref[...].astype(BF16)) + b_ref[...]


def _adaln(c_rows, w_ada, b_ada):
    r, d = c_rows.shape
    n = w_ada.shape[1]
    tn = _divisor_tile(n, 1024, LANES)
    return pl.pallas_call(
        _ada_kernel,
        grid=(n // tn,),
        in_specs=[pl.BlockSpec((r, d), lambda j: (0, 0)),
                  pl.BlockSpec((d, tn), lambda j: (0, j)),
                  pl.BlockSpec((1, tn), lambda j: (0, j))],
        out_specs=pl.BlockSpec((r, tn), lambda j: (0, j)),
        out_shape=jax.ShapeDtypeStruct((r, n), F32),
        compiler_params=_params("parallel"),
    )(c_rows, w_ada, b_ada.reshape(1, n))


def _rows_of(xp_ref, xs_ref, n_prompt_tiles):
    return jnp.where(pl.program_id(0) < n_prompt_tiles, xp_ref[...], xs_ref[...])


def _norm_kernel(xp_ref, xs_ref, g_ref, sc_ref, sh_ref, o_ref, *, n_prompt_tiles):
    x = _rows_of(xp_ref, xs_ref, n_prompt_tiles)
    rows = x.shape[0]
    y = _rms(x) * g_ref[...]
    y = y * (1.0 + _expand_mod(sc_ref[0], rows)) + _expand_mod(sh_ref[0], rows)
    o_ref[...] = y.astype(o_ref.dtype)


def _conv_kernel(h_ref, wh_ref, wb_ref, wc_ref, wconv_ref, hist_ref, z_ref, st_ref, carry_ref, *,
                 tiles_per_seq, n_prompt_tiles, nb):
    i = pl.program_id(0)
    j = pl.program_id(1)
    h = h_ref[...]
    u = _dot(h, wc_ref[...]) * _dot(h, wh_ref[...])
    bg = _dot(h, wb_ref[...])
    tm = u.shape[0]
    w = wconv_ref[...]

    @pl.when(i < n_prompt_tiles)
    def _():
        @pl.when(i % tiles_per_seq == 0)
        def _():
            carry_ref[j] = jnp.zeros(carry_ref.shape[1:], F32)

        tail = carry_ref[j]
        row8 = lax.broadcasted_iota(jnp.int32, (SUBLANES, 1), 0)

        def shifted(k):
            ur = pltpu.roll(u, k, 0)
            top = jnp.where(row8 < k, pltpu.roll(tail, k, 0), ur[:SUBLANES])
            return jnp.concatenate([top, ur[SUBLANES:]], axis=0)

        y = u * w[2:3] + shifted(2) * w[0:1] + shifted(1) * w[1:2]
        z_ref[...] = (bg * y).astype(z_ref.dtype)
        carry_ref[j] = u[tm - SUBLANES:]

    @pl.when(i >= n_prompt_tiles)
    def _():
        ext = jnp.concatenate([hist_ref[...], u], axis=0)
        y = ext[2 * nb:] * w[2:3] + ext[:tm] * w[0:1] + ext[nb:nb + tm] * w[1:2]
        z_ref[...] = (bg * y).astype(z_ref.dtype)

    st_ref[0] = u[tm - 2 * nb:]


def _q_kernel(h_ref, wqd_ref, gq_ref, wupt_ref, gn_ref, gp_ref, cos_ref, sin_ref, qt_ref, *,
              n_heads, d_nope, d_rope, scale, n_prompt_tiles):
    qd = _dot(h_ref[...], wqd_ref[...])
    qdn = (_rms(qd) * gq_ref[...]).astype(BF16)
    qut = _dot_nt(wupt_ref[...], qdn)
    cos_t = cos_ref[...]
    sin_t = sin_ref[...]
    d_qk = d_nope + d_rope
    qscale = jnp.where(pl.program_id(0) < n_prompt_tiles, scale * LOG2_E, scale)
    for hh in range(n_heads):
        qn = qut[hh * d_nope:(hh + 1) * d_nope]
        qp = qut[n_heads * d_nope + hh * d_rope:n_heads * d_nope + (hh + 1) * d_rope]
        ss = jnp.sum(qn * qn, axis=0, keepdims=True) + jnp.sum(qp * qp, axis=0, keepdims=True)
        r = lax.rsqrt(ss / d_qk + EPS)
        qn = qn * r * gn_ref[...]
        qp = _rope_cols(qp * r * gp_ref[...], cos_t, sin_t)
        qt_ref[hh, 0, :d_nope, :] = (qn * qscale).astype(qt_ref.dtype)
        qt_ref[hh, 0, d_nope:, :] = (qp * qscale).astype(qt_ref.dtype)


def _rope_cols(x, cos_t, sin_t):
    half = x.shape[0] // 2
    x1, x2 = x[:half], x[half:]
    return jnp.concatenate([x1 * cos_t - x2 * sin_t, x2 * cos_t + x1 * sin_t], axis=0)


def _kv_kernel(h_ref, wkvd_ref, wkpe_ref, wkpet_ref, gkv_ref, wukt_ref, wuv_ref, gn_ref, gp_ref, cos_ref, sin_ref,
               ckv_ref, kpe_ref, kt_ref, v_ref, *, n_heads, d_nope, d_rope):
    h = h_ref[...]
    ckv = _rms(_dot(h, wkvd_ref[...])) * gkv_ref[...]
    ckv_ref[...] = ckv
    kpe_ref[...] = _dot(h, wkpe_ref[...])
    cb = ckv.astype(BF16)
    v_ref[...] = _dot(cb, wuv_ref[...]).astype(v_ref.dtype)
    knt = _dot_nt(wukt_ref[...], cb)
    kpt = _dot_nt(wkpet_ref[...], h)
    sp = jnp.sum(kpt * kpt, axis=0, keepdims=True)
    krt = _rope_cols(kpt * gp_ref[...], cos_ref[...], sin_ref[...])
    d_qk = d_nope + d_rope
    for hh in range(n_heads):
        kn = knt[hh * d_nope:(hh + 1) * d_nope]
        r = lax.rsqrt((jnp.sum(kn * kn, axis=0, keepdims=True) + sp) / d_qk + EPS)
        kt_ref[hh, 0, :d_nope, :] = (kn * r * gn_ref[...]).astype(kt_ref.dtype)
        kt_ref[hh, 0, d_nope:, :] = (krt * r).astype(kt_ref.dtype)


def _flash_kernel(q_ref, kt_ref, v_ref, o_ref, vext_ref, m_ref, acc_ref, *, tq):
    qi = pl.program_id(2)
    dv = v_ref.shape[1]

    @pl.when(qi == 0)
    def _():
        vext_ref[:, :dv] = v_ref[...]
        vext_ref[:, dv:] = jnp.ones((vext_ref.shape[0], vext_ref.shape[1] - dv), vext_ref.dtype)

    q = q_ref[0]
    m_ref[...] = jnp.full(m_ref.shape, NEG_INF, F32)
    acc_ref[...] = jnp.zeros(acc_ref.shape, F32)
    reps_s = tq // m_ref.shape[1]
    reps_a = acc_ref.shape[1] // m_ref.shape[1]

    def chunk(c, diagonal):
        s = _dot(q, kt_ref[0, c])
        if diagonal:
            row = lax.broadcasted_iota(jnp.int32, s.shape, 0)
            col = lax.broadcasted_iota(jnp.int32, s.shape, 1)
            s = jnp.where(col <= row, s, NEG_INF)
        m_prev = m_ref[...]
        m_new = jnp.maximum(m_prev, jnp.max(s, axis=-1, keepdims=True))
        a = jnp.exp2(m_prev - m_new)
        p = jnp.exp2(s - jnp.concatenate([m_new] * reps_s, axis=1))
        rows = pl.ds(pl.multiple_of(c * tq, tq), tq)
        acc_ref[...] = (jnp.concatenate([a] * reps_a, axis=1) * acc_ref[...]
                        + _dot(p.astype(BF16), vext_ref[rows, :]))
        m_ref[...] = m_new

    def body(c, carry):
        chunk(c, False)
        return carry

    lax.fori_loop(0, qi, body, 0)
    chunk(qi, True)
    acc = acc_ref[...]
    o_ref[...] = (acc[:, :dv] / acc[:, dv:2 * dv]).astype(o_ref.dtype)


def _qabs_kernel(qn_ref, gn_ref, wuk_ref, o_ref):
    q = (qn_ref[0].astype(F32) * gn_ref[...]).astype(BF16)
    o_ref[0] = _dot_nt(q, wuk_ref[...]).astype(o_ref.dtype)


def _paged_kernel(pt_ref, qt_ref, qpe_ref, *refs, pps, layer, n_heads, d_qk, t_new):
    (ckv_hbm, kpe_hbm, cos_ref, sin_ref, cn_ref, kn_ref, cosn_ref, sinn_ref, maskn_ref, wukt_ref, gp_ref,
     o_ref, m_ref, l_ref, acc_ref, cbp_ref, krtp_ref, rinvp_ref, ckv_buf, kpe_buf, sem) = refs
    p = pl.program_id(1)
    n_p = pl.num_programs(1)
    g = pl.program_id(0) * n_p + p
    slot = g % 2

    def page_copies(step, s):
        out = []
        for k in range(pps):
            pg = pt_ref[step * pps + k]
            out.append(pltpu.make_async_copy(ckv_hbm.at[layer, pg], ckv_buf.at[s, k], sem.at[s, 0]))
            out.append(pltpu.make_async_copy(kpe_hbm.at[layer, pg], kpe_buf.at[s, k], sem.at[s, 1]))
        return out

    @pl.when(g == 0)
    def _():
        for c in page_copies(0, 0):
            c.start()

    @pl.when(g + 1 < pl.num_programs(0) * n_p)
    def _():
        for c in page_copies(g + 1, 1 - slot):
            c.start()

    pltpu.make_async_copy(ckv_hbm.at[layer, pl.ds(0, pps)], ckv_buf.at[slot], sem.at[slot, 0]).wait()
    pltpu.make_async_copy(kpe_hbm.at[layer, pl.ds(0, pps)], kpe_buf.at[slot], sem.at[slot, 1]).wait()

    @pl.when(p == 0)
    def _():
        m_ref[...] = jnp.full(m_ref.shape, NEG_INF, F32)
        l_ref[...] = jnp.zeros(l_ref.shape, F32)
        acc_ref[...] = jnp.zeros(acc_ref.shape, F32)
        cbp_ref[...] = jnp.zeros(cbp_ref.shape, cbp_ref.dtype)
        krtp_ref[...] = jnp.zeros(krtp_ref.shape, krtp_ref.dtype)
        rinvp_ref[...] = jnp.zeros(rinvp_ref.shape, F32)

    qt = qt_ref[0]
    qpe = qpe_ref[0]
    d_nope = wukt_ref.shape[0] // n_heads

    def key_stats(cb, kpt, cos_t, sin_t):
        nk = cb.shape[0]
        kt = _dot_nt(wukt_ref[...], cb)
        ssn = jnp.sum((kt * kt).reshape(n_heads, d_nope, nk), axis=1)
        ssp = jnp.sum(kpt * kpt, axis=0, keepdims=True)
        rinv = lax.rsqrt((ssn + ssp) / d_qk + EPS)
        krt = _rope_cols(kpt * gp_ref[...], cos_t, sin_t).astype(BF16)
        return rinv, krt

    def scores(cb, krt, rinv):
        return (_dot_nt(qt, cb) + _dot(qpe, krt)) * jnp.concatenate([rinv] * t_new, axis=0)

    def accumulate(s, cb, weight=None):
        m_prev = m_ref[...]
        m_new = jnp.maximum(m_prev, jnp.max(s, axis=-1, keepdims=True))
        a = jnp.exp(m_prev - m_new)
        e = jnp.exp(s - m_new)
        if weight is not None:
            e = e * weight
        l_ref[...] = a * l_ref[...] + jnp.sum(e, axis=-1, keepdims=True)
        acc_ref[...] = a * acc_ref[...] + _dot(e.astype(BF16), cb)
        m_ref[...] = m_new

    cur = p % 2
    prev = 1 - cur
    page, kvr = ckv_buf.shape[2], ckv_buf.shape[3]
    cbp_ref[cur] = ckv_buf[slot].reshape(pps * page, kvr).astype(BF16)
    cb = cbp_ref[cur]
    cb_prev = cbp_ref[prev]
    s_prev = scores(cb_prev, krtp_ref[prev], rinvp_ref[prev])
    kpt = jnp.concatenate([kpe_buf[slot, k] for k in range(pps)], axis=1)
    rinv, krt = key_stats(cb, kpt, cos_ref[...], sin_ref[...])
    accumulate(jnp.where(p > 0, s_prev, NEG_INF), cb_prev, jnp.where(p > 0, 1.0, 0.0))
    krtp_ref[cur] = krt
    rinvp_ref[cur] = rinv

    @pl.when(p == pl.num_programs(1) - 1)
    def _():
        accumulate(scores(cb, krt, rinv), cb)
        pad = maskn_ref.shape[1] - cn_ref.shape[1]
        cb_new = jnp.concatenate([cn_ref[0], jnp.zeros((pad, cn_ref.shape[2]), F32)], axis=0).astype(BF16)
        rinv_new, krt_new = key_stats(cb_new, kn_ref[0], cosn_ref[...], sinn_ref[...])
        s_new = jnp.where(maskn_ref[...] > 0.0, scores(cb_new, krt_new, rinv_new), NEG_INF)
        accumulate(s_new, cb_new)
        o_ref[0] = (acc_ref[...] / l_ref[...]).astype(o_ref.dtype)


def _ouv_kernel(ol_ref, wuv_ref, o_ref):
    o_ref[...] = _dot(ol_ref[0], wuv_ref[...]).astype(o_ref.dtype)


def _mix_kernel(h_ref, z_ref, op_ref, os_ref, wga_ref, wgb_ref, wco_ref, wao_ref, m_ref, *, n_prompt_tiles):
    h = h_ref[...]
    ga = _dot(h, wga_ref[...])
    gb = _dot(h, wgb_ref[...])
    y_conv = _dot(z_ref[...], wco_ref[...])
    o = jnp.where(pl.program_id(0) < n_prompt_tiles, op_ref[...], os_ref[...])
    y_attn = _dot(o, wao_ref[...])
    m_ref[...] = (_sigmoid(ga) * y_conv + _sigmoid(gb) * y_attn).astype(m_ref.dtype)


def _wo_kernel(m_ref, wo_ref, xp_ref, xs_ref, gt_ref, g_ref, sc_ref, sh_ref, wr_ref, br_ref, gid_ref,
               x1_ref, h2_ref, eid_ref, ew_ref, cnt_ref, run_ref, *, n_groups, n_prompt_tiles):
    rows = xp_ref.shape[0]

    @pl.when(pl.program_id(0) == 0)
    def _():
        run_ref[...] = jnp.zeros(run_ref.shape, F32)

    x1 = (_rows_of(xp_ref, xs_ref, n_prompt_tiles)
          + _expand_mod(gt_ref[0], rows) * _dot(m_ref[...], wo_ref[...]))
    x1_ref[...] = x1
    h2 = _rms(x1) * g_ref[...]
    h2 = h2 * (1.0 + _expand_mod(sc_ref[0], rows)) + _expand_mod(sh_ref[0], rows)
    h2_ref[...] = h2
    logits = _dot(h2.astype(BF16), wr_ref[...]) + br_ref[...]
    icol = lax.broadcasted_iota(jnp.int32, logits.shape, 1)
    col = icol.astype(F32)
    gid = gid_ref[...]
    big = float(logits.shape[1])
    lg = jnp.where(col < n_groups, logits, NEG_INF)
    mg = jnp.max(lg, axis=-1, keepdims=True)
    g_star = jnp.min(jnp.where(lg == mg, col, big), axis=-1, keepdims=True)
    p_top = 1.0 / jnp.sum(jnp.exp(lg - mg), axis=-1, keepdims=True)
    le = jnp.where(gid == g_star, logits, NEG_INF)
    v1 = jnp.max(le, axis=-1, keepdims=True)
    i1 = jnp.min(jnp.where(le == v1, col, big), axis=-1, keepdims=True)
    le2 = jnp.where(col == i1, NEG_INF, le)
    v2 = jnp.max(le2, axis=-1, keepdims=True)
    i2 = jnp.min(jnp.where(le2 == v2, col, big), axis=-1, keepdims=True)
    e2 = jnp.exp(v2 - v1)
    w1 = p_top * (1.0 / (1.0 + e2))
    w2 = p_top * (e2 / (1.0 + e2))
    oh1 = col == i1
    oh2 = col == i2
    onehot = jnp.where(jnp.logical_or(oh1, oh2), 1.0, 0.0)
    r_io = lax.broadcasted_iota(jnp.int32, (rows, rows), 0)
    c_io = lax.broadcasted_iota(jnp.int32, (rows, rows), 1)
    tri = jnp.where(c_io < r_io, 1.0, 0.0).astype(BF16)
    before = _dot(tri, onehot.astype(BF16)) + run_ref[...]
    rank1 = jnp.sum(jnp.where(oh1, before, 0.0), axis=-1, keepdims=True)
    rank2 = jnp.sum(jnp.where(oh2, before, 0.0), axis=-1, keepdims=True)
    run_ref[...] = run_ref[...] + jnp.sum(onehot, axis=0, keepdims=True)
    cnt_ref[...] = jnp.broadcast_to(run_ref[...], cnt_ref.shape)
    meta = jnp.where(icol == 0, i1 - n_groups, jnp.where(icol == 1, i2 - n_groups,
                     jnp.where(icol == 2, rank1, jnp.where(icol == 3, rank2, 0.0))))
    eid_ref[...] = meta.astype(jnp.int32)
    ew_ref[...] = jnp.where(icol == 0, w1, jnp.where(icol == 1, w2, 0.0))


def _row_gather(idx_ref, base, n, src_hbm, dst, sem):
    def body(r, carry):
        tok = idx_ref[base + r]
        pltpu.make_async_copy(src_hbm.at[pl.ds(tok, 1), :], dst.at[pl.ds(r, 1), :], sem).start()
        return carry
    lax.fori_loop(0, n, body, 0, unroll=8)


def _rows_wait(n, src_hbm, dst, sem):
    pltpu.make_async_copy(src_hbm.at[pl.ds(0, n), :], dst, sem).wait()


def _expert_changed(be_ref, b):
    return jnp.logical_or(b == 0, be_ref[b] != be_ref[jnp.maximum(b - 1, 0)])


def _moe_up_kernel(be_ref, tok_ref, h_hbm, wg_ref, wu_ref, a_ref, xbuf, wg_bf, wu_bf, sem):
    b = pl.program_id(0)
    nb = pl.num_programs(0)
    blk = xbuf.shape[1]
    slot = b % 2

    @pl.when(_expert_changed(be_ref, b))
    def _():
        wg_bf[...] = wg_ref[0].astype(BF16)
        wu_bf[...] = wu_ref[0].astype(BF16)

    @pl.when(b == 0)
    def _():
        _row_gather(tok_ref, 0, blk, h_hbm, xbuf.at[0], sem.at[0])

    @pl.when(b + 1 < nb)
    def _():
        _row_gather(tok_ref, (b + 1) * blk, blk, h_hbm, xbuf.at[1 - slot], sem.at[1 - slot])

    _rows_wait(blk, h_hbm, xbuf.at[slot], sem.at[slot])
    x = xbuf[slot].astype(BF16)
    g = _dot(x, wg_bf[...])
    u = _dot(x, wu_bf[...])
    a_ref[...] = (g * _sigmoid(g) * u).astype(a_ref.dtype)


def _moe_down_kernel(be_ref, a_ref, wd_ref, y_ref, wd_bf):
    @pl.when(_expert_changed(be_ref, pl.program_id(0)))
    def _():
        wd_bf[...] = wd_ref[0].astype(BF16)

    y_ref[...] = _dot(a_ref[...], wd_bf[...])


def _combine_kernel(d0_ref, d1_ref, y_hbm, x1_ref, gt_ref, ew_ref, op_ref, os_ref, ybuf, sem, *,
                    n_prompt_steps):
    i = pl.program_id(0)
    n = pl.num_programs(0)
    rows = x1_ref.shape[0]
    slot = i % 2

    def start(step, s):
        _row_gather(d0_ref, step * rows, rows, y_hbm, ybuf.at[s, 0], sem.at[s])
        _row_gather(d1_ref, step * rows, rows, y_hbm, ybuf.at[s, 1], sem.at[s])

    @pl.when(i == 0)
    def _():
        start(0, 0)

    @pl.when(i + 1 < n)
    def _():
        start(i + 1, 1 - slot)

    _rows_wait(rows, y_hbm, ybuf.at[slot, 0], sem.at[slot])
    _rows_wait(rows, y_hbm, ybuf.at[slot, 1], sem.at[slot])
    ew = ew_ref[...]
    y = ybuf[slot, 0] * ew[:, 0:1] + ybuf[slot, 1] * ew[:, 1:2]
    out = x1_ref[...] + _expand_mod(gt_ref[0], rows) * y

    @pl.when(i < n_prompt_steps)
    def _():
        op_ref[...] = out

    @pl.when(i >= n_prompt_steps)
    def _():
        os_ref[...] = out


def _rope_tables(pos, d_rope):
    half = d_rope // 2
    inv = ROPE_BASE ** (-jnp.arange(half, dtype=F32) / half)
    ang = pos.astype(F32)[:, None] * inv[None, :]
    cos, sin = jnp.cos(ang), jnp.sin(ang)
    return jnp.concatenate([cos, cos], axis=1), jnp.concatenate([-sin, sin], axis=1)


def _layer(x_p, x_s, dims, mod_all, state_conv_l, cache_ckv, cache_kpe, page_table, l, P):
    (B, S, NB, T) = dims
    D = x_p.shape[1]
    RP, RS = B * S, T * NB
    R = RP + RS
    C = P['w_conv'].shape[1]
    QR = P['g_q_lat'].shape[0]
    KVR = P['g_kv_lat'].shape[0]
    H, DN = P['w_uk'].shape[1], P['w_uk'].shape[2]
    DV = P['w_uv'].shape[2]
    DR = cache_kpe.shape[-1]
    DQK = DN + DR
    scale = DQK ** -0.5
    PAGE = cache_ckv.shape[2]
    NPAGES = page_table.shape[1]
    PAST = NPAGES * PAGE
    G = P['w_route_group'].shape[1]
    E = P['w_route_expert'].shape[1]
    F = P['w_gate_e'].shape[2]
    MG = NB

    tm = _divisor_tile(int(np.gcd(S, RS)), ROW_TILE, MG)
    assert tm == RS
    n_tiles, np_tiles = R // tm, RP // tm
    xp_spec = pl.BlockSpec((tm, D), lambda i, *_: (jnp.minimum(i, np_tiles - 1), 0))
    xs_spec = pl.BlockSpec((tm, D), lambda i, *_: (0, 0))

    def mod_spec(chunk):
        def idx(i, *_):
            return (jnp.where(i < np_tiles, (i * tm) // S, B), 0, chunk)
        return pl.BlockSpec((1, MG, D), idx)

    def row_spec(width, tile=tm):
        return pl.BlockSpec((tile, width), lambda i, *_: (i, 0))

    def full_spec(shape):
        nd = len(shape)
        return pl.BlockSpec(shape, lambda *_: (0,) * nd)

    offs = np.cumsum([0, C, C, C, QR, KVR, DR, D, D])
    w_h, w_b, w_c, w_qd, w_kvd, w_kpe, w_ga, w_gb = [
        P['w_in'][:, offs[k]:offs[k + 1]].astype(BF16) for k in range(8)]
    w_up = P['w_q_up'].reshape(QR, H, DQK)
    w_up = jnp.concatenate([w_up[:, :, :DN].reshape(QR, H * DN), w_up[:, :, DN:].reshape(QR, H * DR)],
                           axis=1).astype(BF16)
    w_uk = P['w_uk'].reshape(KVR, H * DN).astype(BF16)
    w_uv = P['w_uv'].reshape(KVR, H * DV).astype(BF16)
    w_co = P['w_conv_out'].astype(BF16)
    w_ao = P['w_attn_out'].astype(BF16)
    w_o = P['w_o'].astype(BF16)
    gq_n, gq_p = P['g_qk_q'][:DN].reshape(1, DN), P['g_qk_q'][DN:].reshape(1, DR)
    gk_n, gk_p = P['g_qk_k'][:DN].reshape(1, DN), P['g_qk_k'][DN:].reshape(1, DR)

    pos_rows = jnp.concatenate([jnp.tile(jnp.arange(S), B), PAST + jnp.repeat(jnp.arange(T), NB)])
    cos_rows, sin_rows = _rope_tables(pos_rows, DR)

    h1 = pl.pallas_call(
        functools.partial(_norm_kernel, n_prompt_tiles=np_tiles),
        grid=(n_tiles,),
        in_specs=[xp_spec, xs_spec, full_spec((1, D)), mod_spec(1), mod_spec(0)],
        out_specs=row_spec(D),
        out_shape=jax.ShapeDtypeStruct((R, D), BF16),
        compiler_params=_params("parallel"),
    )(x_p, x_s, P['g_norm_mix'].reshape(1, D), mod_all, mod_all)

    assert T >= 2 and S % RS == 0 and RS >= 2 * NB
    tc = _divisor_tile(C, 512, LANES)
    n_c = C // tc
    tiles_per_seq = S // RS
    n_ct = R // RS
    hist_t = state_conv_l.transpose(1, 0, 2).reshape(2 * NB, C)
    z_all, st_all = pl.pallas_call(
        functools.partial(_conv_kernel, tiles_per_seq=tiles_per_seq, n_prompt_tiles=RP // RS, nb=NB),
        grid=(n_ct, n_c),
        in_specs=[pl.BlockSpec((RS, D), lambda i, j: (i, 0)),
                  pl.BlockSpec((D, tc), lambda i, j: (0, j)),
                  pl.BlockSpec((D, tc), lambda i, j: (0, j)),
                  pl.BlockSpec((D, tc), lambda i, j: (0, j)),
                  pl.BlockSpec((3, tc), lambda i, j: (0, j)),
                  pl.BlockSpec((2 * NB, tc), lambda i, j: (0, j))],
        out_specs=[pl.BlockSpec((RS, tc), lambda i, j: (i, j)),
                   pl.BlockSpec((1, 2 * NB, tc), lambda i, j: (i, 0, j))],
        out_shape=[jax.ShapeDtypeStruct((R, C), BF16), jax.ShapeDtypeStruct((n_ct, 2 * NB, C), F32)],
        scratch_shapes=[pltpu.VMEM((n_c, SUBLANES, tc), F32)],
        compiler_params=_params("arbitrary", "arbitrary"),
    )(h1, w_h, w_b, w_c, P['w_conv'], hist_t)
    new_conv_p = st_all[tiles_per_seq - 1:RP // RS:tiles_per_seq, 2 * NB - 2:]
    new_conv_s = st_all[n_ct - 1].reshape(2, NB, C).transpose(1, 0, 2)
    sblk = RP // RS

    half = DR // 2
    cos_t = cos_rows[:, :half].T
    sin_t = sin_rows[:, half:].T
    tok_spec = pl.BlockSpec((half, tm), lambda i: (0, i))
    qt_all = pl.pallas_call(
        functools.partial(_q_kernel, n_heads=H, d_nope=DN, d_rope=DR, scale=scale, n_prompt_tiles=np_tiles),
        grid=(n_tiles,),
        in_specs=[row_spec(D), full_spec((D, QR)), full_spec((1, QR)), full_spec((H * DQK, QR)),
                  full_spec((DN, 1)), full_spec((DR, 1)), tok_spec, tok_spec],
        out_specs=pl.BlockSpec((H, 1, DQK, tm), lambda i: (0, i, 0, 0)),
        out_shape=jax.ShapeDtypeStruct((H, n_tiles, DQK, tm), BF16),
        compiler_params=_params("parallel"),
    )(h1, w_qd, P['g_q_lat'].reshape(1, QR), w_up.T, gq_n.reshape(DN, 1), gq_p.reshape(DR, 1), cos_t, sin_t)
    q_all = qt_all.transpose(0, 1, 3, 2).reshape(H, R, DQK)

    w_ukt = w_uk.T
    gk_nc, gk_pc = gk_n.reshape(DN, 1), gk_p.reshape(DR, 1)
    ckv_all, kpe_all, kt_all, v_all = pl.pallas_call(
        functools.partial(_kv_kernel, n_heads=H, d_nope=DN, d_rope=DR),
        grid=(n_tiles,),
        in_specs=[row_spec(D), full_spec((D, KVR)), full_spec((D, DR)), full_spec((DR, D)), full_spec((1, KVR)),
                  full_spec((H * DN, KVR)), full_spec((KVR, H * DV)), full_spec((DN, 1)), full_spec((DR, 1)),
                  pl.BlockSpec((half, tm), lambda i: (0, i)), pl.BlockSpec((half, tm), lambda i: (0, i))],
        out_specs=[row_spec(KVR), row_spec(DR),
                   pl.BlockSpec((H, 1, DQK, tm), lambda i: (0, i, 0, 0)), row_spec(H * DV)],
        out_shape=[jax.ShapeDtypeStruct((R, KVR), F32), jax.ShapeDtypeStruct((R, DR), F32),
                   jax.ShapeDtypeStruct((H, n_tiles, DQK, tm), BF16),
                   jax.ShapeDtypeStruct((R, H * DV), BF16)],
        compiler_params=_params("parallel"),
    )(h1, w_kvd, w_kpe, w_kpe.T, P['g_kv_lat'].reshape(1, KVR), w_ukt, w_uv, gk_nc, gk_pc, cos_t, sin_t)

    tq = tm
    nq = S // tq
    o_p = pl.pallas_call(
        functools.partial(_flash_kernel, tq=tq),
        grid=(B, H, nq),
        in_specs=[pl.BlockSpec((1, tq, DQK), lambda b, h, q: (h, b * nq + q, 0)),
                  pl.BlockSpec((1, nq, DQK, tq), lambda b, h, q: (h, b, 0, 0)),
                  pl.BlockSpec((S, DV), lambda b, h, q: (b, h))],
        out_specs=pl.BlockSpec((tq, DV), lambda b, h, q: (b * nq + q, h)),
        out_shape=jax.ShapeDtypeStruct((RP, H * DV), BF16),
        scratch_shapes=[pltpu.VMEM((S, 2 * DV), BF16), pltpu.VMEM((tq, LANES), F32),
                        pltpu.VMEM((tq, 2 * DV), F32)],
        compiler_params=_params("parallel", "parallel", "arbitrary"),
    )(q_all, kt_all, v_all)

    qt = pl.pallas_call(
        _qabs_kernel,
        grid=(H,),
        in_specs=[pl.BlockSpec((1, RS, DN), lambda h: (h, sblk, 0)), full_spec((1, DN)),
                  pl.BlockSpec((KVR, DN), lambda h: (0, h))],
        out_specs=pl.BlockSpec((1, RS, KVR), lambda h: (h, 0, 0)),
        out_shape=jax.ShapeDtypeStruct((H, RS, KVR), BF16),
        compiler_params=_params("parallel"),
    )(q_all, gk_n, w_uk)
    QROWS = T * H
    qt = qt.reshape(H, T, NB, KVR).transpose(2, 1, 0, 3).reshape(NB, QROWS, KVR)
    qpe = q_all[:, RP:, DN:].reshape(H, T, NB, DR).transpose(2, 1, 0, 3).reshape(NB, QROWS, DR)
    pos_keys = jnp.arange(PAST + PAGE)
    cos_k, sin_k = _rope_tables(pos_keys, DR)
    cos_kt, sin_kt = cos_k[:, :half].T, sin_k[:, half:].T
    t_pad = -(-T // SUBLANES) * SUBLANES
    ckv_s = ckv_all[RP:].reshape(T, NB, KVR).transpose(1, 0, 2)
    kpe_s = kpe_all[RP:].reshape(T, NB, DR).transpose(1, 0, 2)
    cn = jnp.pad(ckv_s, ((0, 0), (0, t_pad - T), (0, 0)))
    knt = jnp.pad(kpe_s.transpose(0, 2, 1), ((0, 0), (0, 0), (0, PAGE - T)))
    key_j = jnp.arange(PAGE)[None, :]
    row_t = (jnp.arange(QROWS) // H)[:, None]
    mask_new = ((key_j <= row_t) & (key_j < T)).astype(F32)
    cache_kpe_t = jnp.swapaxes(cache_kpe, 2, 3)
    pps = _divisor_tile(NPAGES, PAGES_PER_STEP, 1)
    n_steps = NPAGES // pps

    def cfull(shape):
        nd = len(shape)
        return pl.BlockSpec(shape, lambda b, p, pt: (0,) * nd)

    o_lat = pl.pallas_call(
        functools.partial(_paged_kernel, pps=pps, layer=l, n_heads=H, d_qk=DQK, t_new=T),
        grid_spec=pltpu.PrefetchScalarGridSpec(
            num_scalar_prefetch=1,
            grid=(NB, n_steps),
            in_specs=([pl.BlockSpec((1, QROWS, KVR), lambda b, p, pt: (b, 0, 0)),
                       pl.BlockSpec((1, QROWS, DR), lambda b, p, pt: (b, 0, 0)),
                       pl.BlockSpec(memory_space=pl.ANY), pl.BlockSpec(memory_space=pl.ANY)]
                      + [pl.BlockSpec((half, pps * PAGE), lambda b, p, pt: (0, p)),
                         pl.BlockSpec((half, pps * PAGE), lambda b, p, pt: (0, p)),
                         pl.BlockSpec((1, t_pad, KVR), lambda b, p, pt: (b, 0, 0)),
                         pl.BlockSpec((1, DR, PAGE), lambda b, p, pt: (b, 0, 0)),
                         pl.BlockSpec((half, PAGE), lambda b, p, pt: (0, NPAGES)),
                         pl.BlockSpec((half, PAGE), lambda b, p, pt: (0, NPAGES)),
                         cfull((QROWS, PAGE)), cfull((H * DN, KVR)), cfull((DR, 1))]),
            out_specs=pl.BlockSpec((1, QROWS, KVR), lambda b, p, pt: (b, 0, 0)),
            scratch_shapes=[pltpu.VMEM((QROWS, 1), F32), pltpu.VMEM((QROWS, 1), F32),
                            pltpu.VMEM((QROWS, KVR), F32), pltpu.VMEM((2, pps * PAGE, KVR), BF16),
                            pltpu.VMEM((2, DR, pps * PAGE), BF16), pltpu.VMEM((2, H, pps * PAGE), F32),
                            pltpu.VMEM((2, pps, PAGE, KVR), F32), pltpu.VMEM((2, pps, DR, PAGE), F32),
                            pltpu.SemaphoreType.DMA((2, 2))]),
        out_shape=jax.ShapeDtypeStruct((NB, QROWS, KVR), BF16),
        compiler_params=_params("arbitrary", "arbitrary"),
    )(page_table.reshape(NB * NPAGES), qt, qpe, cache_ckv, cache_kpe_t, cos_kt, sin_kt,
      cn, knt, cos_kt, sin_kt, mask_new, w_ukt, gk_pc)
    o_lat = o_lat.reshape(NB, T, H, KVR).transpose(2, 1, 0, 3).reshape(H, RS, KVR)
    o_s = pl.pallas_call(
        _ouv_kernel,
        grid=(H,),
        in_specs=[pl.BlockSpec((1, RS, KVR), lambda h: (h, 0, 0)),
                  pl.BlockSpec((KVR, DV), lambda h: (0, h))],
        out_specs=pl.BlockSpec((RS, DV), lambda h: (0, h)),
        out_shape=jax.ShapeDtypeStruct((RS, H * DV), BF16),
        compiler_params=_params("parallel"),
    )(o_lat, w_uv)

    tn = _divisor_tile(D, 512, LANES)
    tmm = _divisor_tile(int(np.gcd(RP, RS)), 1024, SUBLANES)
    npm = RP // tmm
    m_all = pl.pallas_call(
        functools.partial(_mix_kernel, n_prompt_tiles=npm),
        grid=(R // tmm, D // tn),
        in_specs=[pl.BlockSpec((tmm, D), lambda i, j: (i, 0)),
                  pl.BlockSpec((tmm, C), lambda i, j: (i, 0)),
                  pl.BlockSpec((tmm, H * DV), lambda i, j: (jnp.minimum(i, npm - 1), 0)),
                  pl.BlockSpec((tmm, H * DV), lambda i, j: (jnp.maximum(i - npm, 0), 0)),
                  pl.BlockSpec((D, tn), lambda i, j: (0, j)),
                  pl.BlockSpec((D, tn), lambda i, j: (0, j)),
                  pl.BlockSpec((C, tn), lambda i, j: (0, j)),
                  pl.BlockSpec((H * DV, tn), lambda i, j: (0, j))],
        out_specs=pl.BlockSpec((tmm, tn), lambda i, j: (i, j)),
        out_shape=jax.ShapeDtypeStruct((R, D), BF16),
        compiler_params=_params("parallel", "parallel"),
    )(h1, z_all, o_p, o_s, w_ga, w_gb, w_co, w_ao)

    ncol = -(-(G + E) // LANES) * LANES
    w_r = jnp.pad(jnp.concatenate([P['w_route_group'], P['w_route_expert']], axis=1),
                  ((0, 0), (0, ncol - G - E))).astype(BF16)
    b_r = jnp.pad(jnp.concatenate([P['b_route_group'], P['b_route_expert']]), (0, ncol - G - E)).reshape(1, ncol)
    colv = np.arange(ncol)
    gid = np.where((colv >= G) & (colv < G + E), (colv - G) // (E // G), -1).astype(np.float32).reshape(1, ncol)
    x1_all, h2_all, meta_all, ew_all, cnt = pl.pallas_call(
        functools.partial(_wo_kernel, n_groups=G, n_prompt_tiles=np_tiles),
        grid=(n_tiles,),
        in_specs=[row_spec(D), full_spec((D, D)), xp_spec, xs_spec, mod_spec(2), full_spec((1, D)),
                  mod_spec(4), mod_spec(3), full_spec((D, ncol)), full_spec((1, ncol)), full_spec((1, ncol))],
        out_specs=[row_spec(D), row_spec(D), row_spec(ncol), row_spec(ncol), full_spec((SUBLANES, ncol))],
        out_shape=[jax.ShapeDtypeStruct((R, D), F32), jax.ShapeDtypeStruct((R, D), F32),
                   jax.ShapeDtypeStruct((R, ncol), jnp.int32), jax.ShapeDtypeStruct((R, ncol), F32),
                   jax.ShapeDtypeStruct((SUBLANES, ncol), F32)],
        scratch_shapes=[pltpu.VMEM((1, ncol), F32)],
        compiler_params=_params("arbitrary"),
    )(m_all, w_o, x_p, x_s, mod_all, P['g_norm_ffn'].reshape(1, D), mod_all, mod_all, w_r, b_r, jnp.asarray(gid))

    nk = R * TOP_K
    eid = meta_all[:, :TOP_K]
    rank = meta_all[:, TOP_K:2 * TOP_K]
    counts = cnt[0, G:G + E].astype(jnp.int32)
    padded = (counts + MOE_BLOCK - 1) // MOE_BLOCK * MOE_BLOCK
    pad_end = jnp.cumsum(padded)
    pad_start = pad_end - padded
    dest = (pad_start[eid] + rank).astype(jnp.int32)
    n_blocks = (nk + E * (MOE_BLOCK - 1) + MOE_BLOCK - 1) // MOE_BLOCK
    n_slots = n_blocks * MOE_BLOCK
    flat_t = jnp.repeat(jnp.arange(R, dtype=jnp.int32), TOP_K)
    slot_tok = jnp.zeros((n_slots,), jnp.int32).at[dest.reshape(nk)].set(flat_t, unique_indices=True)
    block_start = jnp.arange(n_blocks, dtype=jnp.int32) * MOE_BLOCK
    block_exp = jnp.minimum(jnp.sum(pad_end[None, :] <= block_start[:, None], axis=1), E - 1).astype(jnp.int32)

    a_all = pl.pallas_call(
        _moe_up_kernel,
        grid_spec=pltpu.PrefetchScalarGridSpec(
            num_scalar_prefetch=2,
            grid=(n_blocks,),
            in_specs=[pl.BlockSpec(memory_space=pl.ANY),
                      pl.BlockSpec((1, D, F), lambda b, be, tok: (be[b], 0, 0)),
                      pl.BlockSpec((1, D, F), lambda b, be, tok: (be[b], 0, 0))],
            out_specs=pl.BlockSpec((MOE_BLOCK, F), lambda b, be, tok: (b, 0)),
            scratch_shapes=[pltpu.VMEM((2, MOE_BLOCK, D), F32), pltpu.VMEM((D, F), BF16),
                            pltpu.VMEM((D, F), BF16), pltpu.SemaphoreType.DMA((2,))]),
        out_shape=jax.ShapeDtypeStruct((n_slots, F), BF16),
        compiler_params=_params("arbitrary"),
    )(block_exp, slot_tok, h2_all, P['w_gate_e'], P['w_up_e'])

    y_slots = pl.pallas_call(
        _moe_down_kernel,
        grid_spec=pltpu.PrefetchScalarGridSpec(
            num_scalar_prefetch=1,
            grid=(n_blocks,),
            in_specs=[pl.BlockSpec((MOE_BLOCK, F), lambda b, be: (b, 0)),
                      pl.BlockSpec((1, F, D), lambda b, be: (be[b], 0, 0))],
            out_specs=pl.BlockSpec((MOE_BLOCK, D), lambda b, be: (b, 0)),
            scratch_shapes=[pltpu.VMEM((F, D), BF16)]),
        out_shape=jax.ShapeDtypeStruct((n_slots, D), F32),
        compiler_params=_params("arbitrary"),
    )(block_exp, a_all, P['w_down_e'])

    npc = RP // MG
    y_p, y_s = pl.pallas_call(
        functools.partial(_combine_kernel, n_prompt_steps=npc),
        grid_spec=pltpu.PrefetchScalarGridSpec(
            num_scalar_prefetch=2,
            grid=(R // MG,),
            in_specs=[pl.BlockSpec(memory_space=pl.ANY),
                      pl.BlockSpec((MG, D), lambda i, d0, d1: (i, 0)),
                      pl.BlockSpec((1, MG, D),
                                   lambda i, d0, d1: (jnp.where(i * MG < RP, (i * MG) // S, B), 0, 5)),
                      pl.BlockSpec((MG, ncol), lambda i, d0, d1: (i, 0))],
            out_specs=[pl.BlockSpec((MG, D), lambda i, d0, d1: (jnp.minimum(i, npc - 1), 0)),
                       pl.BlockSpec((MG, D), lambda i, d0, d1: (jnp.maximum(i - npc, 0), 0))],
            scratch_shapes=[pltpu.VMEM((2, TOP_K, MG, D), F32), pltpu.SemaphoreType.DMA((2,))]),
        out_shape=[jax.ShapeDtypeStruct((RP, D), F32), jax.ShapeDtypeStruct((RS, D), F32)],
        compiler_params=_params("arbitrary"),
    )(dest[:, 0], dest[:, 1], y_slots, x1_all, mod_all, ew_all)

    return y_p, y_s, new_conv_p, new_conv_s, ckv_all, kpe_all


def kernel(x_prompt, x_sample, state_conv, cache_ckv, cache_kpe, page_table, c_prompt, c_sample, w_ada, b_ada, g_norm_mix, g_norm_ffn, w_in, w_conv, w_conv_out, g_q_lat, w_q_up, g_kv_lat, w_uk, w_uv, g_qk_q, g_qk_k, w_attn_out, w_o, w_route_group, b_route_group, w_route_expert, b_route_expert, w_gate_e, w_up_e, w_down_e):
    B, S, D = x_prompt.shape
    NB, T, _ = x_sample.shape
    depth = w_in.shape[0]
    RP = B * S
    KVR = cache_ckv.shape[-1]
    DR = cache_kpe.shape[-1]
    C = w_conv.shape[-1]
    x_p = x_prompt.reshape(RP, D)
    x_s = x_sample.transpose(1, 0, 2).reshape(T * NB, D)
    c_rows = jnp.concatenate([jnp.repeat(c_prompt, NB, axis=0), c_sample], axis=0)
    weights = dict(w_ada=w_ada, b_ada=b_ada, g_norm_mix=g_norm_mix, g_norm_ffn=g_norm_ffn, w_in=w_in,
                   w_conv=w_conv, w_conv_out=w_conv_out, g_q_lat=g_q_lat, w_q_up=w_q_up, g_kv_lat=g_kv_lat,
                   w_uk=w_uk, w_uv=w_uv, g_qk_q=g_qk_q, g_qk_k=g_qk_k, w_attn_out=w_attn_out, w_o=w_o,
                   w_route_group=w_route_group, b_route_group=b_route_group, w_route_expert=w_route_expert,
                   b_route_expert=b_route_expert, w_gate_e=w_gate_e, w_up_e=w_up_e, w_down_e=w_down_e)
    conv_p, conv_s, ckv_p, kpe_p, ckv_s, kpe_s = [], [], [], [], [], []
    for l in range(depth):
        P = {k: v[l] for k, v in weights.items()}
        mod_all = _adaln(c_rows, P['w_ada'], P['b_ada']).reshape(B + 1, NB, 6 * D)
        x_p, x_s, cp, cs, ckv_all, kpe_all = _layer(x_p, x_s, (B, S, NB, T), mod_all, state_conv[l], cache_ckv,
                                                    cache_kpe, page_table, l, P)
        conv_p.append(cp)
        conv_s.append(cs)
        ckv_p.append(ckv_all[:RP].reshape(B, S, KVR))
        kpe_p.append(kpe_all[:RP].reshape(B, S, DR))
        ckv_s.append(ckv_all[RP:].reshape(T, NB, KVR).transpose(1, 0, 2))
        kpe_s.append(kpe_all[RP:].reshape(T, NB, DR).transpose(1, 0, 2))
    y_prompt = x_p.reshape(B, S, D)
    y_sample = x_s.reshape(T, NB, D).transpose(1, 0, 2)
    return (y_prompt, y_sample, jnp.stack(conv_p), jnp.stack(conv_s), jnp.stack(ckv_p), jnp.stack(kpe_p),
            jnp.stack(ckv_s), jnp.stack(kpe_s))
```

```python
import functools

import jax
import jax.numpy as jnp
import numpy as np
from jax import lax
from jax.experimental import pallas as pl
from jax.experimental.pallas import tpu as pltpu

EPS = 1e-6
ROPE_BASE = 10000.0
NEG_INF = -1e30
LOG2_E = 1.4426950408889634
TOP_K = 2
BF16 = jnp.bfloat16
F32 = jnp.float32
VMEM_LIMIT_BYTES = 56 * 1024 * 1024
LANES = 128
SUBLANES = 8
ROW_TILE = 512
MOE_BLOCK = 128
PAGES_PER_STEP = 8


def _params(*sem):
    return pltpu.CompilerParams(dimension_semantics=sem, vmem_limit_bytes=VMEM_LIMIT_BYTES)


def _divisor_tile(n, pref, mult):
    best = None
    t = mult
    while t <= min(n, pref):
        if n % t == 0:
            best = t
        t += mult
    return best if best is not None else n


def _dot(a, b):
    return jnp.dot(a, b, preferred_element_type=F32)


def _dot_nt(a, b):
    return lax.dot_general(a, b, (((1,), (1,)), ((), ())), preferred_element_type=F32)


def _sigmoid(x):
    return 1.0 / (1.0 + jnp.exp(-x))


def _rms(x):
    return x * lax.rsqrt(jnp.mean(x * x, axis=-1, keepdims=True) + EPS)


def _expand_mod(m, rows):
    reps = rows // m.shape[0]
    return m if reps == 1 else jnp.concatenate([m] * reps, axis=0)


def _rope_rows(x, cos2, sin2):
    half = x.shape[1] // 2
    swapped = jnp.concatenate([x[:, half:], x[:, :half]], axis=1)
    return x * cos2 + swapped * sin2


def _ada_kernel(c_ref, w_ref, b_ref, o_ref):
    c = c_ref[...]
    s = (c * _sigmoid(c)).astype(BF16)
    o_ref[...] = _dot(s, w_ref[...].astype(BF16)) + b_ref[...]


def _adaln(c_rows, w_ada, b_ada):
    r, d = c_rows.shape
    n = w_ada.shape[1]
    tn = _divisor_tile(n, 1024, LANES)
    return pl.pallas_call(
        _ada_kernel,
        grid=(n // tn,),
        in_specs=[pl.BlockSpec((r, d), lambda j: (0, 0)),
                  pl.BlockSpec((d, tn), lambda j: (0, j)),
                  pl.BlockSpec((1, tn), lambda j: (0, j))],
        out_specs=pl.BlockSpec((r, tn), lambda j: (0, j)),
        out_shape=jax.ShapeDtypeStruct((r, n), F32),
        compiler_params=_params("parallel"),
    )(c_rows, w_ada, b_ada.reshape(1, n))


def _rows_of(xp_ref, xs_ref, n_prompt_tiles):
    return jnp.where(pl.program_id(0) < n_prompt_tiles, xp_ref[...], xs_ref[...])


def _norm_kernel(xp_ref, xs_ref, g_ref, sc_ref, sh_ref, o_ref, *, n_prompt_tiles):
    x = _rows_of(xp_ref, xs_ref, n_prompt_tiles)
    rows = x.shape[0]
    y = _rms(x) * g_ref[...]
    y = y * (1.0 + _expand_mod(sc_ref[0], rows)) + _expand_mod(sh_ref[0], rows)
    o_ref[...] = y.astype(o_ref.dtype)


def _conv_kernel(h_ref, wh_ref, wb_ref, wc_ref, wconv_ref, hist_ref, z_ref, st_ref, carry_ref, *,
                 tiles_per_seq, n_prompt_tiles, nb):
    i = pl.program_id(0)
    j = pl.program_id(1)
    h = h_ref[...]
    u = _dot(h, wc_ref[...]) * _dot(h, wh_ref[...])
    bg = _dot(h, wb_ref[...])
    tm = u.shape[0]
    w = wconv_ref[...]

    @pl.when(i < n_prompt_tiles)
    def _():
        @pl.when(i % tiles_per_seq == 0)
        def _():
            carry_ref[j] = jnp.zeros(carry_ref.shape[1:], F32)

        tail = carry_ref[j]
        row8 = lax.broadcasted_iota(jnp.int32, (SUBLANES, 1), 0)

        def shifted(k):
            ur = pltpu.roll(u, k, 0)
            top = jnp.where(row8 < k, pltpu.roll(tail, k, 0), ur[:SUBLANES])
            return jnp.concatenate([top, ur[SUBLANES:]], axis=0)

        y = u * w[2:3] + shifted(2) * w[0:1] + shifted(1) * w[1:2]
        z_ref[...] = (bg * y).astype(z_ref.dtype)
        carry_ref[j] = u[tm - SUBLANES:]

    @pl.when(i >= n_prompt_tiles)
    def _():
        ext = jnp.concatenate([hist_ref[...], u], axis=0)
        y = ext[2 * nb:] * w[2:3] + ext[:tm] * w[0:1] + ext[nb:nb + tm] * w[1:2]
        z_ref[...] = (bg * y).astype(z_ref.dtype)

    st_ref[0] = u[tm - 2 * nb:]


def _q_kernel(h_ref, wqd_ref, gq_ref, wupt_ref, gn_ref, gp_ref, cos_ref, sin_ref, qt_ref, *,
              n_heads, d_nope, d_rope, scale, n_prompt_tiles):
    qd = _dot(h_ref[...], wqd_ref[...])
    qdn = (_rms(qd) * gq_ref[...]).astype(BF16)
    qut = _dot_nt(wupt_ref[...], qdn)
    cos_t = cos_ref[...]
    sin_t = sin_ref[...]
    d_qk = d_nope + d_rope
    qscale = jnp.where(pl.program_id(0) < n_prompt_tiles, scale * LOG2_E, scale)
    for hh in range(n_heads):
        qn = qut[hh * d_nope:(hh + 1) * d_nope]
        qp = qut[n_heads * d_nope + hh * d_rope:n_heads * d_nope + (hh + 1) * d_rope]
        ss = jnp.sum(qn * qn, axis=0, keepdims=True) + jnp.sum(qp * qp, axis=0, keepdims=True)
        r = lax.rsqrt(ss / d_qk + EPS)
        qn = qn * r * gn_ref[...]
        qp = _rope_cols(qp * r * gp_ref[...], cos_t, sin_t)
        qt_ref[hh, 0, :d_nope, :] = (qn * qscale).astype(qt_ref.dtype)
        qt_ref[hh, 0, d_nope:, :] = (qp * qscale).astype(qt_ref.dtype)


def _rope_cols(x, cos_t, sin_t):
    half = x.shape[0] // 2
    x1, x2 = x[:half], x[half:]
    return jnp.concatenate([x1 * cos_t - x2 * sin_t, x2 * cos_t + x1 * sin_t], axis=0)


def _kv_kernel(h_ref, wkvd_ref, wkpe_ref, wkpet_ref, gkv_ref, wukt_ref, wuv_ref, gn_ref, gp_ref, cos_ref, sin_ref,
               ckv_ref, kpe_ref, kt_ref, v_ref, *, n_heads, d_nope, d_rope):
    h = h_ref[...]
    ckv = _rms(_dot(h, wkvd_ref[...])) * gkv_ref[...]
    ckv_ref[...] = ckv
    kpe_ref[...] = _dot(h, wkpe_ref[...])
    cb = ckv.astype(BF16)
    v_ref[...] = _dot(cb, wuv_ref[...]).astype(v_ref.dtype)
    knt = _dot_nt(wukt_ref[...], cb)
    kpt = _dot_nt(wkpet_ref[...], h)
    sp = jnp.sum(kpt * kpt, axis=0, keepdims=True)
    krt = _rope_cols(kpt * gp_ref[...], cos_ref[...], sin_ref[...])
    d_qk = d_nope + d_rope
    for hh in range(n_heads):
        kn = knt[hh * d_nope:(hh + 1) * d_nope]
        r = lax.rsqrt((jnp.sum(kn * kn, axis=0, keepdims=True) + sp) / d_qk + EPS)
        kt_ref[hh, 0, :d_nope, :] = (kn * r * gn_ref[...]).astype(kt_ref.dtype)
        kt_ref[hh, 0, d_nope:, :] = (krt * r).astype(kt_ref.dtype)


def _flash_kernel(q_ref, kt_ref, v_ref, o_ref, vext_ref, m_ref, acc_ref, *, tq):
    qi = pl.program_id(2)
    dv = v_ref.shape[1]

    @pl.when(qi == 0)
    def _():
        vext_ref[:, :dv] = v_ref[...]
        vext_ref[:, dv:] = jnp.ones((vext_ref.shape[0], vext_ref.shape[1] - dv), vext_ref.dtype)

    q = q_ref[0]
    m_ref[...] = jnp.full(m_ref.shape, NEG_INF, F32)
    acc_ref[...] = jnp.zeros(acc_ref.shape, F32)
    reps_s = tq // m_ref.shape[1]
    reps_a = acc_ref.shape[1] // m_ref.shape[1]

    def chunk(c, diagonal):
        s = _dot(q, kt_ref[0, c])
        if diagonal:
            row = lax.broadcasted_iota(jnp.int32, s.shape, 0)
            col = lax.broadcasted_iota(jnp.int32, s.shape, 1)
            s = jnp.where(col <= row, s, NEG_INF)
        m_prev = m_ref[...]
        m_new = jnp.maximum(m_prev, jnp.max(s, axis=-1, keepdims=True))
        a = jnp.exp2(m_prev - m_new)
        p = jnp.exp2(s - jnp.concatenate([m_new] * reps_s, axis=1))
        rows = pl.ds(pl.multiple_of(c * tq, tq), tq)
        acc_ref[...] = (jnp.concatenate([a] * reps_a, axis=1) * acc_ref[...]
                        + _dot(p.astype(BF16), vext_ref[rows, :]))
        m_ref[...] = m_new

    def body(c, carry):
        chunk(c, False)
        return carry

    lax.fori_loop(0, qi, body, 0)
    chunk(qi, True)
    acc = acc_ref[...]
    o_ref[...] = (acc[:, :dv] / acc[:, dv:2 * dv]).astype(o_ref.dtype)


def _qabs_kernel(qn_ref, gn_ref, wuk_ref, o_ref):
    q = (qn_ref[0].astype(F32) * gn_ref[...]).astype(BF16)
    o_ref[0] = _dot_nt(q, wuk_ref[...]).astype(o_ref.dtype)


def _paged_kernel(pt_ref, qt_ref, qpe_ref, *refs, pps, layer, n_heads, d_qk, t_new):
    (ckv_hbm, kpe_hbm, cos_ref, sin_ref, cn_ref, kn_ref, cosn_ref, sinn_ref, maskn_ref, wukt_ref, gp_ref,
     o_ref, m_ref, l_ref, acc_ref, cbp_ref, krtp_ref, rinvp_ref, ckv_buf, kpe_buf, sem) = refs
    p = pl.program_id(1)
    n_p = pl.num_programs(1)
    g = pl.program_id(0) * n_p + p
    slot = g % 2

    def page_copies(step, s):
        out = []
        for k in range(pps):
            pg = pt_ref[step * pps + k]
            out.append(pltpu.make_async_copy(ckv_hbm.at[layer, pg], ckv_buf.at[s, k], sem.at[s, 0]))
            out.append(pltpu.make_async_copy(kpe_hbm.at[layer, pg], kpe_buf.at[s, k], sem.at[s, 1]))
        return out

    @pl.when(g == 0)
    def _():
        for c in page_copies(0, 0):
            c.start()

    @pl.when(g + 1 < pl.num_programs(0) * n_p)
    def _():
        for c in page_copies(g + 1, 1 - slot):
            c.start()

    pltpu.make_async_copy(ckv_hbm.at[layer, pl.ds(0, pps)], ckv_buf.at[slot], sem.at[slot, 0]).wait()
    pltpu.make_async_copy(kpe_hbm.at[layer, pl.ds(0, pps)], kpe_buf.at[slot], sem.at[slot, 1]).wait()

    @pl.when(p == 0)
    def _():
        m_ref[...] = jnp.full(m_ref.shape, NEG_INF, F32)
        l_ref[...] = jnp.zeros(l_ref.shape, F32)
        acc_ref[...] = jnp.zeros(acc_ref.shape, F32)
        cbp_ref[...] = jnp.zeros(cbp_ref.shape, cbp_ref.dtype)
        krtp_ref[...] = jnp.zeros(krtp_ref.shape, krtp_ref.dtype)
        rinvp_ref[...] = jnp.zeros(rinvp_ref.shape, F32)

    qt = qt_ref[0]
    qpe = qpe_ref[0]
    d_nope = wukt_ref.shape[0] // n_heads

    def key_stats(cb, kpt, cos_t, sin_t):
        nk = cb.shape[0]
        kt = _dot_nt(wukt_ref[...], cb)
        ssn = jnp.sum((kt * kt).reshape(n_heads, d_nope, nk), axis=1)
        ssp = jnp.sum(kpt * kpt, axis=0, keepdims=True)
        rinv = lax.rsqrt((ssn + ssp) / d_qk + EPS)
        krt = _rope_cols(kpt * gp_ref[...], cos_t, sin_t).astype(BF16)
        return rinv, krt

    def scores(cb, krt, rinv):
        return (_dot_nt(qt, cb) + _dot(qpe, krt)) * jnp.concatenate([rinv] * t_new, axis=0)

    def accumulate(s, cb, weight=None):
        m_prev = m_ref[...]
        m_new = jnp.maximum(m_prev, jnp.max(s, axis=-1, keepdims=True))
        a = jnp.exp(m_prev - m_new)
        e = jnp.exp(s - m_new)
        if weight is not None:
            e = e * weight
        l_ref[...] = a * l_ref[...] + jnp.sum(e, axis=-1, keepdims=True)
        acc_ref[...] = a * acc_ref[...] + _dot(e.astype(BF16), cb)
        m_ref[...] = m_new

    cur = p % 2
    prev = 1 - cur
    page, kvr = ckv_buf.shape[2], ckv_buf.shape[3]
    cbp_ref[cur] = ckv_buf[slot].reshape(pps * page, kvr).astype(BF16)
    cb = cbp_ref[cur]
    cb_prev = cbp_ref[prev]
    s_prev = scores(cb_prev, krtp_ref[prev], rinvp_ref[prev])
    kpt = jnp.concatenate([kpe_buf[slot, k] for k in range(pps)], axis=1)
    rinv, krt = key_stats(cb, kpt, cos_ref[...], sin_ref[...])
    accumulate(jnp.where(p > 0, s_prev, NEG_INF), cb_prev, jnp.where(p > 0, 1.0, 0.0))
    krtp_ref[cur] = krt
    rinvp_ref[cur] = rinv

    @pl.when(p == pl.num_programs(1) - 1)
    def _():
        accumulate(scores(cb, krt, rinv), cb)
        pad = maskn_ref.shape[1] - cn_ref.shape[1]
        cb_new = jnp.concatenate([cn_ref[0], jnp.zeros((pad, cn_ref.shape[2]), F32)], axis=0).astype(BF16)
        rinv_new, krt_new = key_stats(cb_new, kn_ref[0], cosn_ref[...], sinn_ref[...])
        s_new = jnp.where(maskn_ref[...] > 0.0, scores(cb_new, krt_new, rinv_new), NEG_INF)
        accumulate(s_new, cb_new)
        o_ref[0] = (acc_ref[...] / l_ref[...]).astype(o_ref.dtype)


def _ouv_kernel(ol_ref, wuv_ref, o_ref):
    o_ref[...] = _dot(ol_ref[0], wuv_ref[...]).astype(o_ref.dtype)


def _mix_kernel(h_ref, z_ref, op_ref, os_ref, wga_ref, wgb_ref, wco_ref, wao_ref, m_ref, *, n_prompt_tiles):
    h = h_ref[...]
    ga = _dot(h, wga_ref[...])
    gb = _dot(h, wgb_ref[...])
    y_conv = _dot(z_ref[...], wco_ref[...])
    o = jnp.where(pl.program_id(0) < n_prompt_tiles, op_ref[...], os_ref[...])
    y_attn = _dot(o, wao_ref[...])
    m_ref[...] = (_sigmoid(ga) * y_conv + _sigmoid(gb) * y_attn).astype(m_ref.dtype)


def _wo_kernel(m_ref, wo_ref, xp_ref, xs_ref, gt_ref, g_ref, sc_ref, sh_ref, wr_ref, br_ref, gid_ref,
               x1_ref, h2_ref, eid_ref, ew_ref, cnt_ref, run_ref, *, n_groups, n_prompt_tiles):
    rows = xp_ref.shape[0]

    @pl.when(pl.program_id(0) == 0)
    def _():
        run_ref[...] = jnp.zeros(run_ref.shape, F32)

    x1 = (_rows_of(xp_ref, xs_ref, n_prompt_tiles)
          + _expand_mod(gt_ref[0], rows) * _dot(m_ref[...], wo_ref[...]))
    x1_ref[...] = x1
    h2 = _rms(x1) * g_ref[...]
    h2 = h2 * (1.0 + _expand_mod(sc_ref[0], rows)) + _expand_mod(sh_ref[0], rows)
    h2_ref[...] = h2
    logits = _dot(h2.astype(BF16), wr_ref[...]) + br_ref[...]
    icol = lax.broadcasted_iota(jnp.int32, logits.shape, 1)
    col = icol.astype(F32)
    gid = gid_ref[...]
    big = float(logits.shape[1])
    lg = jnp.where(col < n_groups, logits, NEG_INF)
    mg = jnp.max(lg, axis=-1, keepdims=True)
    g_star = jnp.min(jnp.where(lg == mg, col, big), axis=-1, keepdims=True)
    p_top = 1.0 / jnp.sum(jnp.exp(lg - mg), axis=-1, keepdims=True)
    le = jnp.where(gid == g_star, logits, NEG_INF)
    v1 = jnp.max(le, axis=-1, keepdims=True)
    i1 = jnp.min(jnp.where(le == v1, col, big), axis=-1, keepdims=True)
    le2 = jnp.where(col == i1, NEG_INF, le)
    v2 = jnp.max(le2, axis=-1, keepdims=True)
    i2 = jnp.min(jnp.where(le2 == v2, col, big), axis=-1, keepdims=True)
    e2 = jnp.exp(v2 - v1)
    w1 = p_top * (1.0 / (1.0 + e2))
    w2 = p_top * (e2 / (1.0 + e2))
    oh1 = col == i1
    oh2 = col == i2
    onehot = jnp.where(jnp.logical_or(oh1, oh2), 1.0, 0.0)
    r_io = lax.broadcasted_iota(jnp.int32, (rows, rows), 0)
    c_io = lax.broadcasted_iota(jnp.int32, (rows, rows), 1)
    tri = jnp.where(c_io < r_io, 1.0, 0.0).astype(BF16)
    before = _dot(tri, onehot.astype(BF16)) + run_ref[...]
    rank1 = jnp.sum(jnp.where(oh1, before, 0.0), axis=-1, keepdims=True)
    rank2 = jnp.sum(jnp.where(oh2, before, 0.0), axis=-1, keepdims=True)
    run_ref[...] = run_ref[...] + jnp.sum(onehot, axis=0, keepdims=True)
    cnt_ref[...] = jnp.broadcast_to(run_ref[...], cnt_ref.shape)
    meta = jnp.where(icol == 0, i1 - n_groups, jnp.where(icol == 1, i2 - n_groups,
                     jnp.where(icol == 2, rank1, jnp.where(icol == 3, rank2, 0.0))))
    eid_ref[...] = meta.astype(jnp.int32)
    ew_ref[...] = jnp.where(icol == 0, w1, jnp.where(icol == 1, w2, 0.0))


def _row_gather(idx_ref, base, n, src_hbm, dst, sem):
    def body(r, carry):
        tok = idx_ref[base + r]
        pltpu.make_async_copy(src_hbm.at[pl.ds(tok, 1), :], dst.at[pl.ds(r, 1), :], sem).start()
        return carry
    lax.fori_loop(0, n, body, 0, unroll=8)


def _rows_wait(n, src_hbm, dst, sem):
    pltpu.make_async_copy(src_hbm.at[pl.ds(0, n), :], dst, sem).wait()


def _expert_weights(b, be_ref, first_ref, nxt_ref, par_ref, w_hbms, w_stage, w_bfs, wsem):
    def copies(e, s):
        return [pltpu.make_async_copy(w.at[e], w_stage.at[s, k], wsem.at[s]) for k, w in enumerate(w_hbms)]

    @pl.when(b == 0)
    def _():
        for c in copies(be_ref[0], 0):
            c.start()

    @pl.when(first_ref[b] == 1)
    def _():
        s = par_ref[b]
        for c in copies(be_ref[b], s):
            c.wait()
        for k, w_bf in enumerate(w_bfs):
            w_bf[...] = w_stage[s, k].astype(BF16)

        @pl.when(nxt_ref[b] != be_ref[b])
        def _():
            for c in copies(nxt_ref[b], 1 - s):
                c.start()


def _moe_up_kernel(be_ref, first_ref, nxt_ref, par_ref, tok_ref, h_hbm, wg_hbm, wu_hbm, a_ref,
                   xbuf, w_stage, wg_bf, wu_bf, sem, wsem):
    b = pl.program_id(0)
    nb = pl.num_programs(0)
    blk = xbuf.shape[1]
    slot = b % 2
    _expert_weights(b, be_ref, first_ref, nxt_ref, par_ref, (wg_hbm, wu_hbm), w_stage, (wg_bf, wu_bf), wsem)

    @pl.when(b == 0)
    def _():
        _row_gather(tok_ref, 0, blk, h_hbm, xbuf.at[0], sem.at[0])

    @pl.when(b + 1 < nb)
    def _():
        _row_gather(tok_ref, (b + 1) * blk, blk, h_hbm, xbuf.at[1 - slot], sem.at[1 - slot])

    _rows_wait(blk, h_hbm, xbuf.at[slot], sem.at[slot])
    x = xbuf[slot].astype(BF16)
    g = _dot(x, wg_bf[...])
    u = _dot(x, wu_bf[...])
    a_ref[...] = (g * _sigmoid(g) * u).astype(a_ref.dtype)


def _moe_down_kernel(be_ref, first_ref, nxt_ref, par_ref, a_ref, wd_hbm, y_ref, w_stage, wd_bf, wsem):
    _expert_weights(pl.program_id(0), be_ref, first_ref, nxt_ref, par_ref, (wd_hbm,), w_stage, (wd_bf,), wsem)
    y_ref[...] = _dot(a_ref[...], wd_bf[...])


def _combine_kernel(d0_ref, d1_ref, y_hbm, x1_ref, gt_ref, ew_ref, op_ref, os_ref, ybuf, sem, *,
                    n_prompt_steps):
    i = pl.program_id(0)
    n = pl.num_programs(0)
    rows = x1_ref.shape[0]
    slot = i % 2

    def start(step, s):
        _row_gather(d0_ref, step * rows, rows, y_hbm, ybuf.at[s, 0], sem.at[s])
        _row_gather(d1_ref, step * rows, rows, y_hbm, ybuf.at[s, 1], sem.at[s])

    @pl.when(i == 0)
    def _():
        start(0, 0)

    @pl.when(i + 1 < n)
    def _():
        start(i + 1, 1 - slot)

    _rows_wait(rows, y_hbm, ybuf.at[slot, 0], sem.at[slot])
    _rows_wait(rows, y_hbm, ybuf.at[slot, 1], sem.at[slot])
    ew = ew_ref[...]
    y = ybuf[slot, 0] * ew[:, 0:1] + ybuf[slot, 1] * ew[:, 1:2]
    out = x1_ref[...] + _expand_mod(gt_ref[0], rows) * y

    @pl.when(i < n_prompt_steps)
    def _():
        op_ref[...] = out

    @pl.when(i >= n_prompt_steps)
    def _():
        os_ref[...] = out


def _rope_tables(pos, d_rope):
    half = d_rope // 2
    inv = ROPE_BASE ** (-jnp.arange(half, dtype=F32) / half)
    ang = pos.astype(F32)[:, None] * inv[None, :]
    cos, sin = jnp.cos(ang), jnp.sin(ang)
    return jnp.concatenate([cos, cos], axis=1), jnp.concatenate([-sin, sin], axis=1)


def _layer(x_p, x_s, dims, mod_all, state_conv_l, cache_ckv, cache_kpe, page_table, l, P):
    (B, S, NB, T) = dims
    D = x_p.shape[1]
    RP, RS = B * S, T * NB
    R = RP + RS
    C = P['w_conv'].shape[1]
    QR = P['g_q_lat'].shape[0]
    KVR = P['g_kv_lat'].shape[0]
    H, DN = P['w_uk'].shape[1], P['w_uk'].shape[2]
    DV = P['w_uv'].shape[2]
    DR = cache_kpe.shape[-1]
    DQK = DN + DR
    scale = DQK ** -0.5
    PAGE = cache_ckv.shape[2]
    NPAGES = page_table.shape[1]
    PAST = NPAGES * PAGE
    G = P['w_route_group'].shape[1]
    E = P['w_route_expert'].shape[1]
    F = P['w_gate_e'].shape[2]
    MG = NB

    tm = _divisor_tile(int(np.gcd(S, RS)), ROW_TILE, MG)
    assert tm == RS
    n_tiles, np_tiles = R // tm, RP // tm
    xp_spec = pl.BlockSpec((tm, D), lambda i, *_: (jnp.minimum(i, np_tiles - 1), 0))
    xs_spec = pl.BlockSpec((tm, D), lambda i, *_: (0, 0))

    def mod_spec(chunk):
        def idx(i, *_):
            return (jnp.where(i < np_tiles, (i * tm) // S, B), 0, chunk)
        return pl.BlockSpec((1, MG, D), idx)

    def row_spec(width, tile=tm):
        return pl.BlockSpec((tile, width), lambda i, *_: (i, 0))

    def full_spec(shape):
        nd = len(shape)
        return pl.BlockSpec(shape, lambda *_: (0,) * nd)

    offs = np.cumsum([0, C, C, C, QR, KVR, DR, D, D])
    w_h, w_b, w_c, w_qd, w_kvd, w_kpe, w_ga, w_gb = [
        P['w_in'][:, offs[k]:offs[k + 1]].astype(BF16) for k in range(8)]
    w_up = P['w_q_up'].reshape(QR, H, DQK)
    w_up = jnp.concatenate([w_up[:, :, :DN].reshape(QR, H * DN), w_up[:, :, DN:].reshape(QR, H * DR)],
                           axis=1).astype(BF16)
    w_uk = P['w_uk'].reshape(KVR, H * DN).astype(BF16)
    w_uv = P['w_uv'].reshape(KVR, H * DV).astype(BF16)
    w_co = P['w_conv_out'].astype(BF16)
    w_ao = P['w_attn_out'].astype(BF16)
    w_o = P['w_o'].astype(BF16)
    gq_n, gq_p = P['g_qk_q'][:DN].reshape(1, DN), P['g_qk_q'][DN:].reshape(1, DR)
    gk_n, gk_p = P['g_qk_k'][:DN].reshape(1, DN), P['g_qk_k'][DN:].reshape(1, DR)

    pos_rows = jnp.concatenate([jnp.tile(jnp.arange(S), B), PAST + jnp.repeat(jnp.arange(T), NB)])
    cos_rows, sin_rows = _rope_tables(pos_rows, DR)

    h1 = pl.pallas_call(
        functools.partial(_norm_kernel, n_prompt_tiles=np_tiles),
        grid=(n_tiles,),
        in_specs=[xp_spec, xs_spec, full_spec((1, D)), mod_spec(1), mod_spec(0)],
        out_specs=row_spec(D),
        out_shape=jax.ShapeDtypeStruct((R, D), BF16),
        compiler_params=_params("parallel"),
    )(x_p, x_s, P['g_norm_mix'].reshape(1, D), mod_all, mod_all)

    assert T >= 2 and S % RS == 0 and RS >= 2 * NB
    tc = _divisor_tile(C, 512, LANES)
    n_c = C // tc
    tiles_per_seq = S // RS
    n_ct = R // RS
    hist_t = state_conv_l.transpose(1, 0, 2).reshape(2 * NB, C)
    z_all, st_all = pl.pallas_call(
        functools.partial(_conv_kernel, tiles_per_seq=tiles_per_seq, n_prompt_tiles=RP // RS, nb=NB),
        grid=(n_ct, n_c),
        in_specs=[pl.BlockSpec((RS, D), lambda i, j: (i, 0)),
                  pl.BlockSpec((D, tc), lambda i, j: (0, j)),
                  pl.BlockSpec((D, tc), lambda i, j: (0, j)),
                  pl.BlockSpec((D, tc), lambda i, j: (0, j)),
                  pl.BlockSpec((3, tc), lambda i, j: (0, j)),
                  pl.BlockSpec((2 * NB, tc), lambda i, j: (0, j))],
        out_specs=[pl.BlockSpec((RS, tc), lambda i, j: (i, j)),
                   pl.BlockSpec((1, 2 * NB, tc), lambda i, j: (i, 0, j))],
        out_shape=[jax.ShapeDtypeStruct((R, C), BF16), jax.ShapeDtypeStruct((n_ct, 2 * NB, C), F32)],
        scratch_shapes=[pltpu.VMEM((n_c, SUBLANES, tc), F32)],
        compiler_params=_params("arbitrary", "arbitrary"),
    )(h1, w_h, w_b, w_c, P['w_conv'], hist_t)
    new_conv_p = st_all[tiles_per_seq - 1:RP // RS:tiles_per_seq, 2 * NB - 2:]
    new_conv_s = st_all[n_ct - 1].reshape(2, NB, C).transpose(1, 0, 2)
    sblk = RP // RS

    half = DR // 2
    cos_t = cos_rows[:, :half].T
    sin_t = sin_rows[:, half:].T
    tok_spec = pl.BlockSpec((half, tm), lambda i: (0, i))
    qt_all = pl.pallas_call(
        functools.partial(_q_kernel, n_heads=H, d_nope=DN, d_rope=DR, scale=scale, n_prompt_tiles=np_tiles),
        grid=(n_tiles,),
        in_specs=[row_spec(D), full_spec((D, QR)), full_spec((1, QR)), full_spec((H * DQK, QR)),
                  full_spec((DN, 1)), full_spec((DR, 1)), tok_spec, tok_spec],
        out_specs=pl.BlockSpec((H, 1, DQK, tm), lambda i: (0, i, 0, 0)),
        out_shape=jax.ShapeDtypeStruct((H, n_tiles, DQK, tm), BF16),
        compiler_params=_params("parallel"),
    )(h1, w_qd, P['g_q_lat'].reshape(1, QR), w_up.T, gq_n.reshape(DN, 1), gq_p.reshape(DR, 1), cos_t, sin_t)
    q_all = qt_all.transpose(0, 1, 3, 2).reshape(H, R, DQK)

    w_ukt = w_uk.T
    gk_nc, gk_pc = gk_n.reshape(DN, 1), gk_p.reshape(DR, 1)
    ckv_all, kpe_all, kt_all, v_all = pl.pallas_call(
        functools.partial(_kv_kernel, n_heads=H, d_nope=DN, d_rope=DR),
        grid=(n_tiles,),
        in_specs=[row_spec(D), full_spec((D, KVR)), full_spec((D, DR)), full_spec((DR, D)), full_spec((1, KVR)),
                  full_spec((H * DN, KVR)), full_spec((KVR, H * DV)), full_spec((DN, 1)), full_spec((DR, 1)),
                  pl.BlockSpec((half, tm), lambda i: (0, i)), pl.BlockSpec((half, tm), lambda i: (0, i))],
        out_specs=[row_spec(KVR), row_spec(DR),
                   pl.BlockSpec((H, 1, DQK, tm), lambda i: (0, i, 0, 0)), row_spec(H * DV)],
        out_shape=[jax.ShapeDtypeStruct((R, KVR), F32), jax.ShapeDtypeStruct((R, DR), F32),
                   jax.ShapeDtypeStruct((H, n_tiles, DQK, tm), BF16),
                   jax.ShapeDtypeStruct((R, H * DV), BF16)],
        compiler_params=_params("parallel"),
    )(h1, w_kvd, w_kpe, w_kpe.T, P['g_kv_lat'].reshape(1, KVR), w_ukt, w_uv, gk_nc, gk_pc, cos_t, sin_t)

    tq = tm
    nq = S // tq
    o_p = pl.pallas_call(
        functools.partial(_flash_kernel, tq=tq),
        grid=(B, H, nq),
        in_specs=[pl.BlockSpec((1, tq, DQK), lambda b, h, q: (h, b * nq + q, 0)),
                  pl.BlockSpec((1, nq, DQK, tq), lambda b, h, q: (h, b, 0, 0)),
                  pl.BlockSpec((S, DV), lambda b, h, q: (b, h))],
        out_specs=pl.BlockSpec((tq, DV), lambda b, h, q: (b * nq + q, h)),
        out_shape=jax.ShapeDtypeStruct((RP, H * DV), BF16),
        scratch_shapes=[pltpu.VMEM((S, 2 * DV), BF16), pltpu.VMEM((tq, LANES), F32),
                        pltpu.VMEM((tq, 2 * DV), F32)],
        compiler_params=_params("parallel", "parallel", "arbitrary"),
    )(q_all, kt_all, v_all)

    qt = pl.pallas_call(
        _qabs_kernel,
        grid=(H,),
        in_specs=[pl.BlockSpec((1, RS, DN), lambda h: (h, sblk, 0)), full_spec((1, DN)),
                  pl.BlockSpec((KVR, DN), lambda h: (0, h))],
        out_specs=pl.BlockSpec((1, RS, KVR), lambda h: (h, 0, 0)),
        out_shape=jax.ShapeDtypeStruct((H, RS, KVR), BF16),
        compiler_params=_params("parallel"),
    )(q_all, gk_n, w_uk)
    QROWS = T * H
    qt = qt.reshape(H, T, NB, KVR).transpose(2, 1, 0, 3).reshape(NB, QROWS, KVR)
    qpe = q_all[:, RP:, DN:].reshape(H, T, NB, DR).transpose(2, 1, 0, 3).reshape(NB, QROWS, DR)
    pos_keys = jnp.arange(PAST + PAGE)
    cos_k, sin_k = _rope_tables(pos_keys, DR)
    cos_kt, sin_kt = cos_k[:, :half].T, sin_k[:, half:].T
    t_pad = -(-T // SUBLANES) * SUBLANES
    ckv_s = ckv_all[RP:].reshape(T, NB, KVR).transpose(1, 0, 2)
    kpe_s = kpe_all[RP:].reshape(T, NB, DR).transpose(1, 0, 2)
    cn = jnp.pad(ckv_s, ((0, 0), (0, t_pad - T), (0, 0)))
    knt = jnp.pad(kpe_s.transpose(0, 2, 1), ((0, 0), (0, 0), (0, PAGE - T)))
    key_j = jnp.arange(PAGE)[None, :]
    row_t = (jnp.arange(QROWS) // H)[:, None]
    mask_new = ((key_j <= row_t) & (key_j < T)).astype(F32)
    cache_kpe_t = jnp.swapaxes(cache_kpe, 2, 3)
    pps = _divisor_tile(NPAGES, PAGES_PER_STEP, 1)
    n_steps = NPAGES // pps

    def cfull(shape):
        nd = len(shape)
        return pl.BlockSpec(shape, lambda b, p, pt: (0,) * nd)

    o_lat = pl.pallas_call(
        functools.partial(_paged_kernel, pps=pps, layer=l, n_heads=H, d_qk=DQK, t_new=T),
        grid_spec=pltpu.PrefetchScalarGridSpec(
            num_scalar_prefetch=1,
            grid=(NB, n_steps),
            in_specs=([pl.BlockSpec((1, QROWS, KVR), lambda b, p, pt: (b, 0, 0)),
                       pl.BlockSpec((1, QROWS, DR), lambda b, p, pt: (b, 0, 0)),
                       pl.BlockSpec(memory_space=pl.ANY), pl.BlockSpec(memory_space=pl.ANY)]
                      + [pl.BlockSpec((half, pps * PAGE), lambda b, p, pt: (0, p)),
                         pl.BlockSpec((half, pps * PAGE), lambda b, p, pt: (0, p)),
                         pl.BlockSpec((1, t_pad, KVR), lambda b, p, pt: (b, 0, 0)),
                         pl.BlockSpec((1, DR, PAGE), lambda b, p, pt: (b, 0, 0)),
                         pl.BlockSpec((half, PAGE), lambda b, p, pt: (0, NPAGES)),
                         pl.BlockSpec((half, PAGE), lambda b, p, pt: (0, NPAGES)),
                         cfull((QROWS, PAGE)), cfull((H * DN, KVR)), cfull((DR, 1))]),
            out_specs=pl.BlockSpec((1, QROWS, KVR), lambda b, p, pt: (b, 0, 0)),
            scratch_shapes=[pltpu.VMEM((QROWS, 1), F32), pltpu.VMEM((QROWS, 1), F32),
                            pltpu.VMEM((QROWS, KVR), F32), pltpu.VMEM((2, pps * PAGE, KVR), BF16),
                            pltpu.VMEM((2, DR, pps * PAGE), BF16), pltpu.VMEM((2, H, pps * PAGE), F32),
                            pltpu.VMEM((2, pps, PAGE, KVR), F32), pltpu.VMEM((2, pps, DR, PAGE), F32),
                            pltpu.SemaphoreType.DMA((2, 2))]),
        out_shape=jax.ShapeDtypeStruct((NB, QROWS, KVR), BF16),
        compiler_params=_params("arbitrary", "arbitrary"),
    )(page_table.reshape(NB * NPAGES), qt, qpe, cache_ckv, cache_kpe_t, cos_kt, sin_kt,
      cn, knt, cos_kt, sin_kt, mask_new, w_ukt, gk_pc)
    o_lat = o_lat.reshape(NB, T, H, KVR).transpose(2, 1, 0, 3).reshape(H, RS, KVR)
    o_s = pl.pallas_call(
        _ouv_kernel,
        grid=(H,),
        in_specs=[pl.BlockSpec((1, RS, KVR), lambda h: (h, 0, 0)),
                  pl.BlockSpec((KVR, DV), lambda h: (0, h))],
        out_specs=pl.BlockSpec((RS, DV), lambda h: (0, h)),
        out_shape=jax.ShapeDtypeStruct((RS, H * DV), BF16),
        compiler_params=_params("parallel"),
    )(o_lat, w_uv)

    tn = _divisor_tile(D, 512, LANES)
    tmm = _divisor_tile(int(np.gcd(RP, RS)), 1024, SUBLANES)
    npm = RP // tmm
    m_all = pl.pallas_call(
        functools.partial(_mix_kernel, n_prompt_tiles=npm),
        grid=(R // tmm, D // tn),
        in_specs=[pl.BlockSpec((tmm, D), lambda i, j: (i, 0)),
                  pl.BlockSpec((tmm, C), lambda i, j: (i, 0)),
                  pl.BlockSpec((tmm, H * DV), lambda i, j: (jnp.minimum(i, npm - 1), 0)),
                  pl.BlockSpec((tmm, H * DV), lambda i, j: (jnp.maximum(i - npm, 0), 0)),
                  pl.BlockSpec((D, tn), lambda i, j: (0, j)),
                  pl.BlockSpec((D, tn), lambda i, j: (0, j)),
                  pl.BlockSpec((C, tn), lambda i, j: (0, j)),
                  pl.BlockSpec((H * DV, tn), lambda i, j: (0, j))],
        out_specs=pl.BlockSpec((tmm, tn), lambda i, j: (i, j)),
        out_shape=jax.ShapeDtypeStruct((R, D), BF16),
        compiler_params=_params("parallel", "parallel"),
    )(h1, z_all, o_p, o_s, w_ga, w_gb, w_co, w_ao)

    ncol = -(-(G + E) // LANES) * LANES
    w_r = jnp.pad(jnp.concatenate([P['w_route_group'], P['w_route_expert']], axis=1),
                  ((0, 0), (0, ncol - G - E))).astype(BF16)
    b_r = jnp.pad(jnp.concatenate([P['b_route_group'], P['b_route_expert']]), (0, ncol - G - E)).reshape(1, ncol)
    colv = np.arange(ncol)
    gid = np.where((colv >= G) & (colv < G + E), (colv - G) // (E // G), -1).astype(np.float32).reshape(1, ncol)
    x1_all, h2_all, meta_all, ew_all, cnt = pl.pallas_call(
        functools.partial(_wo_kernel, n_groups=G, n_prompt_tiles=np_tiles),
        grid=(n_tiles,),
        in_specs=[row_spec(D), full_spec((D, D)), xp_spec, xs_spec, mod_spec(2), full_spec((1, D)),
                  mod_spec(4), mod_spec(3), full_spec((D, ncol)), full_spec((1, ncol)), full_spec((1, ncol))],
        out_specs=[row_spec(D), row_spec(D), row_spec(ncol), row_spec(ncol), full_spec((SUBLANES, ncol))],
        out_shape=[jax.ShapeDtypeStruct((R, D), F32), jax.ShapeDtypeStruct((R, D), F32),
                   jax.ShapeDtypeStruct((R, ncol), jnp.int32), jax.ShapeDtypeStruct((R, ncol), F32),
                   jax.ShapeDtypeStruct((SUBLANES, ncol), F32)],
        scratch_shapes=[pltpu.VMEM((1, ncol), F32)],
        compiler_params=_params("arbitrary"),
    )(m_all, w_o, x_p, x_s, mod_all, P['g_norm_ffn'].reshape(1, D), mod_all, mod_all, w_r, b_r, jnp.asarray(gid))

    nk = R * TOP_K
    eid = meta_all[:, :TOP_K]
    rank = meta_all[:, TOP_K:2 * TOP_K]
    counts = cnt[0, G:G + E].astype(jnp.int32)
    padded = (counts + MOE_BLOCK - 1) // MOE_BLOCK * MOE_BLOCK
    pad_end = jnp.cumsum(padded)
    pad_start = pad_end - padded
    dest = (pad_start[eid] + rank).astype(jnp.int32)
    n_blocks = (nk + E * (MOE_BLOCK - 1) + MOE_BLOCK - 1) // MOE_BLOCK
    n_slots = n_blocks * MOE_BLOCK
    flat_t = jnp.repeat(jnp.arange(R, dtype=jnp.int32), TOP_K)
    slot_tok = jnp.zeros((n_slots,), jnp.int32).at[dest.reshape(nk)].set(flat_t, unique_indices=True)
    block_start = jnp.arange(n_blocks, dtype=jnp.int32) * MOE_BLOCK
    block_exp = jnp.minimum(jnp.sum(pad_end[None, :] <= block_start[:, None], axis=1), E - 1).astype(jnp.int32)

    first_blk = jnp.concatenate([jnp.ones((1,), jnp.int32),
                                 (block_exp[1:] != block_exp[:-1]).astype(jnp.int32)])
    group_par = ((jnp.cumsum(first_blk) - 1) % 2).astype(jnp.int32)
    next_start = jnp.sum(block_exp[None, :] <= block_exp[:, None], axis=1)
    next_exp = block_exp[jnp.minimum(next_start, n_blocks - 1)]

    a_all = pl.pallas_call(
        _moe_up_kernel,
        grid_spec=pltpu.PrefetchScalarGridSpec(
            num_scalar_prefetch=5,
            grid=(n_blocks,),
            in_specs=[pl.BlockSpec(memory_space=pl.ANY), pl.BlockSpec(memory_space=pl.ANY),
                      pl.BlockSpec(memory_space=pl.ANY)],
            out_specs=pl.BlockSpec((MOE_BLOCK, F), lambda b, *_: (b, 0)),
            scratch_shapes=[pltpu.VMEM((2, MOE_BLOCK, D), F32), pltpu.VMEM((2, 2, D, F), F32),
                            pltpu.VMEM((D, F), BF16), pltpu.VMEM((D, F), BF16),
                            pltpu.SemaphoreType.DMA((2,)), pltpu.SemaphoreType.DMA((2,))]),
        out_shape=jax.ShapeDtypeStruct((n_slots, F), BF16),
        compiler_params=_params("arbitrary"),
    )(block_exp, first_blk, next_exp, group_par, slot_tok, h2_all, P['w_gate_e'], P['w_up_e'])

    y_slots = pl.pallas_call(
        _moe_down_kernel,
        grid_spec=pltpu.PrefetchScalarGridSpec(
            num_scalar_prefetch=4,
            grid=(n_blocks,),
            in_specs=[pl.BlockSpec((MOE_BLOCK, F), lambda b, *_: (b, 0)),
                      pl.BlockSpec(memory_space=pl.ANY)],
            out_specs=pl.BlockSpec((MOE_BLOCK, D), lambda b, *_: (b, 0)),
            scratch_shapes=[pltpu.VMEM((2, 1, F, D), F32), pltpu.VMEM((F, D), BF16),
                            pltpu.SemaphoreType.DMA((2,))]),
        out_shape=jax.ShapeDtypeStruct((n_slots, D), F32),
        compiler_params=_params("arbitrary"),
    )(block_exp, first_blk, next_exp, group_par, a_all, P['w_down_e'])

    npc = RP // MG
    y_p, y_s = pl.pallas_call(
        functools.partial(_combine_kernel, n_prompt_steps=npc),
        grid_spec=pltpu.PrefetchScalarGridSpec(
            num_scalar_prefetch=2,
            grid=(R // MG,),
            in_specs=[pl.BlockSpec(memory_space=pl.ANY),
                      pl.BlockSpec((MG, D), lambda i, d0, d1: (i, 0)),
                      pl.BlockSpec((1, MG, D),
                                   lambda i, d0, d1: (jnp.where(i * MG < RP, (i * MG) // S, B), 0, 5)),
                      pl.BlockSpec((MG, ncol), lambda i, d0, d1: (i, 0))],
            out_specs=[pl.BlockSpec((MG, D), lambda i, d0, d1: (jnp.minimum(i, npc - 1), 0)),
                       pl.BlockSpec((MG, D), lambda i, d0, d1: (jnp.maximum(i - npc, 0), 0))],
            scratch_shapes=[pltpu.VMEM((2, TOP_K, MG, D), F32), pltpu.SemaphoreType.DMA((2,))]),
        out_shape=[jax.ShapeDtypeStruct((RP, D), F32), jax.ShapeDtypeStruct((RS, D), F32)],
        compiler_params=_params("arbitrary"),
    )(dest[:, 0], dest[:, 1], y_slots, x1_all, mod_all, ew_all)

    return y_p, y_s, new_conv_p, new_conv_s, ckv_all, kpe_all


def kernel(x_prompt, x_sample, state_conv, cache_ckv, cache_kpe, page_table, c_prompt, c_sample, w_ada, b_ada, g_norm_mix, g_norm_ffn, w_in, w_conv, w_conv_out, g_q_lat, w_q_up, g_kv_lat, w_uk, w_uv, g_qk_q, g_qk_k, w_attn_out, w_o, w_route_group, b_route_group, w_route_expert, b_route_expert, w_gate_e, w_up_e, w_down_e):
    B, S, D = x_prompt.shape
    NB, T, _ = x_sample.shape
    depth = w_in.shape[0]
    RP = B * S
    KVR = cache_ckv.shape[-1]
    DR = cache_kpe.shape[-1]
    C = w_conv.shape[-1]
    x_p = x_prompt.reshape(RP, D)
    x_s = x_sample.transpose(1, 0, 2).reshape(T * NB, D)
    c_rows = jnp.concatenate([jnp.repeat(c_prompt, NB, axis=0), c_sample], axis=0)
    weights = dict(w_ada=w_ada, b_ada=b_ada, g_norm_mix=g_norm_mix, g_norm_ffn=g_norm_ffn, w_in=w_in,
                   w_conv=w_conv, w_conv_out=w_conv_out, g_q_lat=g_q_lat, w_q_up=w_q_up, g_kv_lat=g_kv_lat,
                   w_uk=w_uk, w_uv=w_uv, g_qk_q=g_qk_q, g_qk_k=g_qk_k, w_attn_out=w_attn_out, w_o=w_o,
                   w_route_group=w_route_group, b_route_group=b_route_group, w_route_expert=w_route_expert,
                   b_route_expert=b_route_expert, w_gate_e=w_gate_e, w_up_e=w_up_e, w_down_e=w_down_e)
    conv_p, conv_s, ckv_p, kpe_p, ckv_s, kpe_s = [], [], [], [], [], []
    for l in range(depth):
        P = {k: v[l] for k, v in weights.items()}
        mod_all = _adaln(c_rows, P['w_ada'], P['b_ada']).reshape(B + 1, NB, 6 * D)
        x_p, x_s, cp, cs, ckv_all, kpe_all = _layer(x_p, x_s, (B, S, NB, T), mod_all, state_conv[l], cache_ckv,
                                                    cache_kpe, page_table, l, P)
        conv_p.append(cp)
        conv_s.append(cs)
        ckv_p.append(ckv_all[:RP].reshape(B, S, KVR))
        kpe_p.append(kpe_all[:RP].reshape(B, S, DR))
        ckv_s.append(ckv_all[RP:].reshape(T, NB, KVR).transpose(1, 0, 2))
        kpe_s.append(kpe_all[RP:].reshape(T, NB, DR).transpose(1, 0, 2))
    y_prompt = x_p.reshape(B, S, D)
    y_sample = x_s.reshape(T, NB, D).transpose(1, 0, 2)
    return (y_prompt, y_sample, jnp.stack(conv_p), jnp.stack(conv_s), jnp.stack(ckv_p), jnp.stack(kpe_p),
            jnp.stack(ckv_s), jnp.stack(kpe_s))
```

```python
import functools

import jax
import jax.numpy as jnp
import numpy as np
from jax import lax
from jax.experimental import pallas as pl
from jax.experimental.pallas import tpu as pltpu

EPS = 1e-6
ROPE_BASE = 10000.0
NEG_INF = -1e30
LOG2_E = 1.4426950408889634
TOP_K = 2
BF16 = jnp.bfloat16
F32 = jnp.float32
VMEM_LIMIT_BYTES = 56 * 1024 * 1024
LANES = 128
SUBLANES = 8
ROW_TILE = 512
MOE_BLOCK = 128
PAGES_PER_STEP = 8


def _params(*sem):
    return pltpu.CompilerParams(dimension_semantics=sem, vmem_limit_bytes=VMEM_LIMIT_BYTES)


def _divisor_tile(n, pref, mult):
    best = None
    t = mult
    while t <= min(n, pref):
        if n % t == 0:
            best = t
        t += mult
    return best if best is not None else n


def _dot(a, b):
    return jnp.dot(a, b, preferred_element_type=F32)


def _dot_nt(a, b):
    return lax.dot_general(a, b, (((1,), (1,)), ((), ())), preferred_element_type=F32)


def _sigmoid(x):
    return 1.0 / (1.0 + jnp.exp(-x))


def _rms(x):
    return x * lax.rsqrt(jnp.mean(x * x, axis=-1, keepdims=True) + EPS)


def _expand_mod(m, rows):
    reps = rows // m.shape[0]
    return m if reps == 1 else jnp.concatenate([m] * reps, axis=0)


def _rope_rows(x, cos2, sin2):
    half = x.shape[1] // 2
    swapped = jnp.concatenate([x[:, half:], x[:, :half]], axis=1)
    return x * cos2 + swapped * sin2


def _ada_kernel(c_ref, w_ref, b_ref, o_ref):
    c = c_ref[...]
    s = (c * _sigmoid(c)).astype(BF16)
    o_ref[...] = _dot(s, w_ref[...].astype(BF16)) + b_ref[...]


def _adaln(c_rows, w_ada, b_ada):
    r, d = c_rows.shape
    n = w_ada.shape[1]
    tn = _divisor_tile(n, 1024, LANES)
    return pl.pallas_call(
        _ada_kernel,
        grid=(n // tn,),
        in_specs=[pl.BlockSpec((r, d), lambda j: (0, 0)),
                  pl.BlockSpec((d, tn), lambda j: (0, j)),
                  pl.BlockSpec((1, tn), lambda j: (0, j))],
        out_specs=pl.BlockSpec((r, tn), lambda j: (0, j)),
        out_shape=jax.ShapeDtypeStruct((r, n), F32),
        compiler_params=_params("parallel"),
    )(c_rows, w_ada, b_ada.reshape(1, n))


def _rows_of(xp_ref, xs_ref, n_prompt_tiles):
    return jnp.where(pl.program_id(0) < n_prompt_tiles, xp_ref[...], xs_ref[...])


def _norm_kernel(xp_ref, xs_ref, g_ref, sc_ref, sh_ref, o_ref, *, n_prompt_tiles):
    x = _rows_of(xp_ref, xs_ref, n_prompt_tiles)
    rows = x.shape[0]
    y = _rms(x) * g_ref[...]
    y = y * (1.0 + _expand_mod(sc_ref[0], rows)) + _expand_mod(sh_ref[0], rows)
    o_ref[...] = y.astype(o_ref.dtype)


def _conv_kernel(h_ref, wh_ref, wb_ref, wc_ref, wconv_ref, hist_ref, z_ref, st_ref, carry_ref, *,
                 tiles_per_seq, n_prompt_tiles, nb):
    i = pl.program_id(0)
    j = pl.program_id(1)
    h = h_ref[...]
    u = _dot(h, wc_ref[...]) * _dot(h, wh_ref[...])
    bg = _dot(h, wb_ref[...])
    tm = u.shape[0]
    w = wconv_ref[...]

    @pl.when(i < n_prompt_tiles)
    def _():
        @pl.when(i % tiles_per_seq == 0)
        def _():
            carry_ref[j] = jnp.zeros(carry_ref.shape[1:], F32)

        tail = carry_ref[j]
        row8 = lax.broadcasted_iota(jnp.int32, (SUBLANES, 1), 0)

        def shifted(k):
            ur = pltpu.roll(u, k, 0)
            top = jnp.where(row8 < k, pltpu.roll(tail, k, 0), ur[:SUBLANES])
            return jnp.concatenate([top, ur[SUBLANES:]], axis=0)

        y = u * w[2:3] + shifted(2) * w[0:1] + shifted(1) * w[1:2]
        z_ref[...] = (bg * y).astype(z_ref.dtype)
        carry_ref[j] = u[tm - SUBLANES:]

    @pl.when(i >= n_prompt_tiles)
    def _():
        ext = jnp.concatenate([hist_ref[...], u], axis=0)
        y = ext[2 * nb:] * w[2:3] + ext[:tm] * w[0:1] + ext[nb:nb + tm] * w[1:2]
        z_ref[...] = (bg * y).astype(z_ref.dtype)

    st_ref[0] = u[tm - 2 * nb:]


def _q_kernel(h_ref, wqd_ref, gq_ref, wupt_ref, gn_ref, gp_ref, cos_ref, sin_ref, qt_ref, *,
              n_heads, d_nope, d_rope, scale, n_prompt_tiles):
    qd = _dot(h_ref[...], wqd_ref[...])
    qdn = (_rms(qd) * gq_ref[...]).astype(BF16)
    qut = _dot_nt(wupt_ref[...], qdn)
    cos_t = cos_ref[...]
    sin_t = sin_ref[...]
    d_qk = d_nope + d_rope
    qscale = jnp.where(pl.program_id(0) < n_prompt_tiles, scale * LOG2_E, scale)
    for hh in range(n_heads):
        qn = qut[hh * d_nope:(hh + 1) * d_nope]
        qp = qut[n_heads * d_nope + hh * d_rope:n_heads * d_nope + (hh + 1) * d_rope]
        ss = jnp.sum(qn * qn, axis=0, keepdims=True) + jnp.sum(qp * qp, axis=0, keepdims=True)
        r = lax.rsqrt(ss / d_qk + EPS)
        qn = qn * r * gn_ref[...]
        qp = _rope_cols(qp * r * gp_ref[...], cos_t, sin_t)
        qt_ref[hh, 0, :d_nope, :] = (qn * qscale).astype(qt_ref.dtype)
        qt_ref[hh, 0, d_nope:, :] = (qp * qscale).astype(qt_ref.dtype)


def _rope_cols(x, cos_t, sin_t):
    half = x.shape[0] // 2
    x1, x2 = x[:half], x[half:]
    return jnp.concatenate([x1 * cos_t - x2 * sin_t, x2 * cos_t + x1 * sin_t], axis=0)


def _kv_kernel(h_ref, wkvd_ref, wkpe_ref, wkpet_ref, gkv_ref, wukt_ref, wuv_ref, gn_ref, gp_ref, cos_ref, sin_ref,
               ckv_ref, kpe_ref, kt_ref, v_ref, *, n_heads, d_nope, d_rope):
    h = h_ref[...]
    ckv = _rms(_dot(h, wkvd_ref[...])) * gkv_ref[...]
    ckv_ref[...] = ckv
    kpe_ref[...] = _dot(h, wkpe_ref[...])
    cb = ckv.astype(BF16)
    v_ref[...] = _dot(cb, wuv_ref[...]).astype(v_ref.dtype)
    knt = _dot_nt(wukt_ref[...], cb)
    kpt = _dot_nt(wkpet_ref[...], h)
    sp = jnp.sum(kpt * kpt, axis=0, keepdims=True)
    krt = _rope_cols(kpt * gp_ref[...], cos_ref[...], sin_ref[...])
    d_qk = d_nope + d_rope
    for hh in range(n_heads):
        kn = knt[hh * d_nope:(hh + 1) * d_nope]
        r = lax.rsqrt((jnp.sum(kn * kn, axis=0, keepdims=True) + sp) / d_qk + EPS)
        kt_ref[hh, 0, :d_nope, :] = (kn * r * gn_ref[...]).astype(kt_ref.dtype)
        kt_ref[hh, 0, d_nope:, :] = (krt * r).astype(kt_ref.dtype)


def _flash_kernel(q_ref, kt_ref, v_ref, o_ref, vext_ref, m_ref, acc_ref, *, tq):
    qi = pl.program_id(2)
    dv = v_ref.shape[1]

    @pl.when(qi == 0)
    def _():
        vext_ref[:, :dv] = v_ref[...]
        vext_ref[:, dv:] = jnp.ones((vext_ref.shape[0], vext_ref.shape[1] - dv), vext_ref.dtype)

    q = q_ref[0]
    m_ref[...] = jnp.full(m_ref.shape, NEG_INF, F32)
    acc_ref[...] = jnp.zeros(acc_ref.shape, F32)
    reps_s = tq // m_ref.shape[1]
    reps_a = acc_ref.shape[1] // m_ref.shape[1]

    def chunk(c, diagonal):
        s = _dot(q, kt_ref[0, c])
        if diagonal:
            row = lax.broadcasted_iota(jnp.int32, s.shape, 0)
            col = lax.broadcasted_iota(jnp.int32, s.shape, 1)
            s = jnp.where(col <= row, s, NEG_INF)
        m_prev = m_ref[...]
        m_new = jnp.maximum(m_prev, jnp.max(s, axis=-1, keepdims=True))
        a = jnp.exp2(m_prev - m_new)
        p = jnp.exp2(s - jnp.concatenate([m_new] * reps_s, axis=1))
        rows = pl.ds(pl.multiple_of(c * tq, tq), tq)
        acc_ref[...] = (jnp.concatenate([a] * reps_a, axis=1) * acc_ref[...]
                        + _dot(p.astype(BF16), vext_ref[rows, :]))
        m_ref[...] = m_new

    def body(c, carry):
        chunk(c, False)
        return carry

    lax.fori_loop(0, qi, body, 0)
    chunk(qi, True)
    acc = acc_ref[...]
    o_ref[...] = (acc[:, :dv] / acc[:, dv:2 * dv]).astype(o_ref.dtype)


def _qabs_kernel(qn_ref, gn_ref, wuk_ref, o_ref):
    q = (qn_ref[0].astype(F32) * gn_ref[...]).astype(BF16)
    o_ref[0] = _dot_nt(q, wuk_ref[...]).astype(o_ref.dtype)


def _paged_kernel(pt_ref, qt_ref, qpe_ref, *refs, pps, layer, n_heads, d_qk, t_new):
    (ckv_hbm, kpe_hbm, cos_ref, sin_ref, cn_ref, kn_ref, cosn_ref, sinn_ref, maskn_ref, wukt_ref, gp_ref,
     o_ref, m_ref, l_ref, acc_ref, cbp_ref, krtp_ref, rinvp_ref, ckv_buf, kpe_buf, sem) = refs
    p = pl.program_id(1)
    n_p = pl.num_programs(1)
    g = pl.program_id(0) * n_p + p
    slot = g % 2

    def page_copies(step, s):
        out = []
        for k in range(pps):
            pg = pt_ref[step * pps + k]
            out.append(pltpu.make_async_copy(ckv_hbm.at[layer, pg], ckv_buf.at[s, k], sem.at[s, 0]))
            out.append(pltpu.make_async_copy(kpe_hbm.at[layer, pg], kpe_buf.at[s, k], sem.at[s, 1]))
        return out

    @pl.when(g == 0)
    def _():
        for c in page_copies(0, 0):
            c.start()

    @pl.when(g + 1 < pl.num_programs(0) * n_p)
    def _():
        for c in page_copies(g + 1, 1 - slot):
            c.start()

    pltpu.make_async_copy(ckv_hbm.at[layer, pl.ds(0, pps)], ckv_buf.at[slot], sem.at[slot, 0]).wait()
    pltpu.make_async_copy(kpe_hbm.at[layer, pl.ds(0, pps)], kpe_buf.at[slot], sem.at[slot, 1]).wait()

    @pl.when(p == 0)
    def _():
        m_ref[...] = jnp.full(m_ref.shape, NEG_INF, F32)
        l_ref[...] = jnp.zeros(l_ref.shape, F32)
        acc_ref[...] = jnp.zeros(acc_ref.shape, F32)
        cbp_ref[...] = jnp.zeros(cbp_ref.shape, cbp_ref.dtype)
        krtp_ref[...] = jnp.zeros(krtp_ref.shape, krtp_ref.dtype)
        rinvp_ref[...] = jnp.zeros(rinvp_ref.shape, F32)

    qt = qt_ref[0]
    qpe = qpe_ref[0]
    d_nope = wukt_ref.shape[0] // n_heads

    def key_stats(cb, kpt, cos_t, sin_t):
        nk = cb.shape[0]
        kt = _dot_nt(wukt_ref[...], cb)
        ssn = jnp.sum((kt * kt).reshape(n_heads, d_nope, nk), axis=1)
        ssp = jnp.sum(kpt * kpt, axis=0, keepdims=True)
        rinv = lax.rsqrt((ssn + ssp) / d_qk + EPS)
        krt = _rope_cols(kpt * gp_ref[...], cos_t, sin_t).astype(BF16)
        return rinv, krt

    def scores(cb, krt, rinv):
        return (_dot_nt(qt, cb) + _dot(qpe, krt)) * jnp.concatenate([rinv] * t_new, axis=0)

    def accumulate(s, cb, weight=None):
        m_prev = m_ref[...]
        m_new = jnp.maximum(m_prev, jnp.max(s, axis=-1, keepdims=True))
        a = jnp.exp(m_prev - m_new)
        e = jnp.exp(s - m_new)
        if weight is not None:
            e = e * weight
        l_ref[...] = a * l_ref[...] + jnp.sum(e, axis=-1, keepdims=True)
        acc_ref[...] = a * acc_ref[...] + _dot(e.astype(BF16), cb)
        m_ref[...] = m_new

    cur = p % 2
    prev = 1 - cur
    page, kvr = ckv_buf.shape[2], ckv_buf.shape[3]
    cbp_ref[cur] = ckv_buf[slot].reshape(pps * page, kvr).astype(BF16)
    cb = cbp_ref[cur]
    cb_prev = cbp_ref[prev]
    s_prev = scores(cb_prev, krtp_ref[prev], rinvp_ref[prev])
    kpt = jnp.concatenate([kpe_buf[slot, k] for k in range(pps)], axis=1)
    rinv, krt = key_stats(cb, kpt, cos_ref[...], sin_ref[...])
    accumulate(jnp.where(p > 0, s_prev, NEG_INF), cb_prev, jnp.where(p > 0, 1.0, 0.0))
    krtp_ref[cur] = krt
    rinvp_ref[cur] = rinv

    @pl.when(p == pl.num_programs(1) - 1)
    def _():
        accumulate(scores(cb, krt, rinv), cb)
        pad = maskn_ref.shape[1] - cn_ref.shape[1]
        cb_new = jnp.concatenate([cn_ref[0], jnp.zeros((pad, cn_ref.shape[2]), F32)], axis=0).astype(BF16)
        rinv_new, krt_new = key_stats(cb_new, kn_ref[0], cosn_ref[...], sinn_ref[...])
        s_new = jnp.where(maskn_ref[...] > 0.0, scores(cb_new, krt_new, rinv_new), NEG_INF)
        accumulate(s_new, cb_new)
        o_ref[0] = (acc_ref[...] / l_ref[...]).astype(o_ref.dtype)


def _ouv_kernel(ol_ref, wuv_ref, o_ref):
    o_ref[...] = _dot(ol_ref[0], wuv_ref[...]).astype(o_ref.dtype)


def _mix_kernel(h_ref, z_ref, op_ref, os_ref, wga_ref, wgb_ref, wco_ref, wao_ref, m_ref, *, n_prompt_tiles):
    h = h_ref[...]
    ga = _dot(h, wga_ref[...])
    gb = _dot(h, wgb_ref[...])
    y_conv = _dot(z_ref[...], wco_ref[...])
    o = jnp.where(pl.program_id(0) < n_prompt_tiles, op_ref[...], os_ref[...])
    y_attn = _dot(o, wao_ref[...])
    m_ref[...] = (_sigmoid(ga) * y_conv + _sigmoid(gb) * y_attn).astype(m_ref.dtype)


def _wo_kernel(m_ref, wo_ref, xp_ref, xs_ref, gt_ref, g_ref, sc_ref, sh_ref, wr_ref, br_ref, gid_ref,
               x1_ref, h2_ref, eid_ref, ew_ref, cnt_ref, run_ref, *, n_groups, n_prompt_tiles):
    rows = xp_ref.shape[0]

    @pl.when(pl.program_id(0) == 0)
    def _():
        run_ref[...] = jnp.zeros(run_ref.shape, F32)

    x1 = (_rows_of(xp_ref, xs_ref, n_prompt_tiles)
          + _expand_mod(gt_ref[0], rows) * _dot(m_ref[...], wo_ref[...]))
    x1_ref[...] = x1
    h2 = _rms(x1) * g_ref[...]
    h2 = h2 * (1.0 + _expand_mod(sc_ref[0], rows)) + _expand_mod(sh_ref[0], rows)
    h2_ref[...] = h2
    logits = _dot(h2.astype(BF16), wr_ref[...]) + br_ref[...]
    icol = lax.broadcasted_iota(jnp.int32, logits.shape, 1)
    col = icol.astype(F32)
    gid = gid_ref[...]
    big = float(logits.shape[1])
    lg = jnp.where(col < n_groups, logits, NEG_INF)
    mg = jnp.max(lg, axis=-1, keepdims=True)
    g_star = jnp.min(jnp.where(lg == mg, col, big), axis=-1, keepdims=True)
    p_top = 1.0 / jnp.sum(jnp.exp(lg - mg), axis=-1, keepdims=True)
    le = jnp.where(gid == g_star, logits, NEG_INF)
    v1 = jnp.max(le, axis=-1, keepdims=True)
    i1 = jnp.min(jnp.where(le == v1, col, big), axis=-1, keepdims=True)
    le2 = jnp.where(col == i1, NEG_INF, le)
    v2 = jnp.max(le2, axis=-1, keepdims=True)
    i2 = jnp.min(jnp.where(le2 == v2, col, big), axis=-1, keepdims=True)
    e2 = jnp.exp(v2 - v1)
    w1 = p_top * (1.0 / (1.0 + e2))
    w2 = p_top * (e2 / (1.0 + e2))
    oh1 = col == i1
    oh2 = col == i2
    onehot = jnp.where(jnp.logical_or(oh1, oh2), 1.0, 0.0)
    r_io = lax.broadcasted_iota(jnp.int32, (rows, rows), 0)
    c_io = lax.broadcasted_iota(jnp.int32, (rows, rows), 1)
    tri = jnp.where(c_io < r_io, 1.0, 0.0).astype(BF16)
    before = _dot(tri, onehot.astype(BF16)) + run_ref[...]
    rank1 = jnp.sum(jnp.where(oh1, before, 0.0), axis=-1, keepdims=True)
    rank2 = jnp.sum(jnp.where(oh2, before, 0.0), axis=-1, keepdims=True)
    run_ref[...] = run_ref[...] + jnp.sum(onehot, axis=0, keepdims=True)
    cnt_ref[...] = jnp.broadcast_to(run_ref[...], cnt_ref.shape)
    meta = jnp.where(icol == 0, i1 - n_groups, jnp.where(icol == 1, i2 - n_groups,
                     jnp.where(icol == 2, rank1, jnp.where(icol == 3, rank2, 0.0))))
    eid_ref[...] = meta.astype(jnp.int32)
    ew_ref[...] = jnp.where(icol == 0, w1, jnp.where(icol == 1, w2, 0.0))


def _row_gather(idx_ref, base, n, src_hbm, dst, sem):
    def body(r, carry):
        tok = idx_ref[base + r]
        pltpu.make_async_copy(src_hbm.at[pl.ds(tok, 1), :], dst.at[pl.ds(r, 1), :], sem).start()
        return carry
    lax.fori_loop(0, n, body, 0, unroll=32)


def _rows_wait(n, src_hbm, dst, sem):
    pltpu.make_async_copy(src_hbm.at[pl.ds(0, n), :], dst, sem).wait()


def _expert_weights(b, be_ref, first_ref, nxt_ref, par_ref, w_hbms, w_stage, w_bfs, wsem):
    def copies(e, s):
        return [pltpu.make_async_copy(w.at[e], w_stage.at[s, k], wsem.at[s]) for k, w in enumerate(w_hbms)]

    @pl.when(b == 0)
    def _():
        for c in copies(be_ref[0], 0):
            c.start()

    @pl.when(first_ref[b] == 1)
    def _():
        s = par_ref[b]
        for c in copies(be_ref[b], s):
            c.wait()
        for k, w_bf in enumerate(w_bfs):
            w_bf[...] = w_stage[s, k].astype(BF16)

        @pl.when(nxt_ref[b] != be_ref[b])
        def _():
            for c in copies(nxt_ref[b], 1 - s):
                c.start()


def _moe_up_kernel(be_ref, first_ref, nxt_ref, par_ref, tok_ref, h_hbm, wg_hbm, wu_hbm, a_ref,
                   xbuf, w_stage, wg_bf, wu_bf, sem, wsem):
    b = pl.program_id(0)
    nb = pl.num_programs(0)
    blk = xbuf.shape[1]
    slot = b % 2
    _expert_weights(b, be_ref, first_ref, nxt_ref, par_ref, (wg_hbm, wu_hbm), w_stage, (wg_bf, wu_bf), wsem)

    @pl.when(b == 0)
    def _():
        _row_gather(tok_ref, 0, blk, h_hbm, xbuf.at[0], sem.at[0])

    @pl.when(b + 1 < nb)
    def _():
        _row_gather(tok_ref, (b + 1) * blk, blk, h_hbm, xbuf.at[1 - slot], sem.at[1 - slot])

    _rows_wait(blk, h_hbm, xbuf.at[slot], sem.at[slot])
    x = xbuf[slot].astype(BF16)
    g = _dot(x, wg_bf[...])
    u = _dot(x, wu_bf[...])
    a_ref[...] = (g * _sigmoid(g) * u).astype(a_ref.dtype)


def _moe_down_kernel(be_ref, first_ref, nxt_ref, par_ref, a_ref, wd_hbm, y_ref, w_stage, wd_bf, wsem):
    _expert_weights(pl.program_id(0), be_ref, first_ref, nxt_ref, par_ref, (wd_hbm,), w_stage, (wd_bf,), wsem)
    y_ref[...] = _dot(a_ref[...], wd_bf[...])


def _combine_kernel(d0_ref, d1_ref, y_hbm, x1_ref, gt_ref, ew_ref, op_ref, os_ref, ybuf, sem, *,
                    n_prompt_steps):
    i = pl.program_id(0)
    n = pl.num_programs(0)
    rows = x1_ref.shape[0]
    slot = i % 2

    def start(step, s):
        _row_gather(d0_ref, step * rows, rows, y_hbm, ybuf.at[s, 0], sem.at[s])
        _row_gather(d1_ref, step * rows, rows, y_hbm, ybuf.at[s, 1], sem.at[s])

    @pl.when(i == 0)
    def _():
        start(0, 0)

    @pl.when(i + 1 < n)
    def _():
        start(i + 1, 1 - slot)

    _rows_wait(rows, y_hbm, ybuf.at[slot, 0], sem.at[slot])
    _rows_wait(rows, y_hbm, ybuf.at[slot, 1], sem.at[slot])
    ew = ew_ref[...]
    y = ybuf[slot, 0] * ew[:, 0:1] + ybuf[slot, 1] * ew[:, 1:2]
    out = x1_ref[...] + _expand_mod(gt_ref[0], rows) * y

    @pl.when(i < n_prompt_steps)
    def _():
        op_ref[...] = out

    @pl.when(i >= n_prompt_steps)
    def _():
        os_ref[...] = out


def _rope_tables(pos, d_rope):
    half = d_rope // 2
    inv = ROPE_BASE ** (-jnp.arange(half, dtype=F32) / half)
    ang = pos.astype(F32)[:, None] * inv[None, :]
    cos, sin = jnp.cos(ang), jnp.sin(ang)
    return jnp.concatenate([cos, cos], axis=1), jnp.concatenate([-sin, sin], axis=1)


def _layer(x_p, x_s, dims, mod_all, state_conv_l, cache_ckv, cache_kpe, page_table, l, P):
    (B, S, NB, T) = dims
    D = x_p.shape[1]
    RP, RS = B * S, T * NB
    R = RP + RS
    C = P['w_conv'].shape[1]
    QR = P['g_q_lat'].shape[0]
    KVR = P['g_kv_lat'].shape[0]
    H, DN = P['w_uk'].shape[1], P['w_uk'].shape[2]
    DV = P['w_uv'].shape[2]
    DR = cache_kpe.shape[-1]
    DQK = DN + DR
    scale = DQK ** -0.5
    PAGE = cache_ckv.shape[2]
    NPAGES = page_table.shape[1]
    PAST = NPAGES * PAGE
    G = P['w_route_group'].shape[1]
    E = P['w_route_expert'].shape[1]
    F = P['w_gate_e'].shape[2]
    MG = NB

    tm = _divisor_tile(int(np.gcd(S, RS)), ROW_TILE, MG)
    assert tm == RS
    n_tiles, np_tiles = R // tm, RP // tm
    xp_spec = pl.BlockSpec((tm, D), lambda i, *_: (jnp.minimum(i, np_tiles - 1), 0))
    xs_spec = pl.BlockSpec((tm, D), lambda i, *_: (0, 0))

    def mod_spec(chunk):
        def idx(i, *_):
            return (jnp.where(i < np_tiles, (i * tm) // S, B), 0, chunk)
        return pl.BlockSpec((1, MG, D), idx)

    def row_spec(width, tile=tm):
        return pl.BlockSpec((tile, width), lambda i, *_: (i, 0))

    def full_spec(shape):
        nd = len(shape)
        return pl.BlockSpec(shape, lambda *_: (0,) * nd)

    offs = np.cumsum([0, C, C, C, QR, KVR, DR, D, D])
    w_h, w_b, w_c, w_qd, w_kvd, w_kpe, w_ga, w_gb = [
        P['w_in'][:, offs[k]:offs[k + 1]].astype(BF16) for k in range(8)]
    w_up = P['w_q_up'].reshape(QR, H, DQK)
    w_up = jnp.concatenate([w_up[:, :, :DN].reshape(QR, H * DN), w_up[:, :, DN:].reshape(QR, H * DR)],
                           axis=1).astype(BF16)
    w_uk = P['w_uk'].reshape(KVR, H * DN).astype(BF16)
    w_uv = P['w_uv'].reshape(KVR, H * DV).astype(BF16)
    w_co = P['w_conv_out'].astype(BF16)
    w_ao = P['w_attn_out'].astype(BF16)
    w_o = P['w_o'].astype(BF16)
    gq_n, gq_p = P['g_qk_q'][:DN].reshape(1, DN), P['g_qk_q'][DN:].reshape(1, DR)
    gk_n, gk_p = P['g_qk_k'][:DN].reshape(1, DN), P['g_qk_k'][DN:].reshape(1, DR)

    pos_rows = jnp.concatenate([jnp.tile(jnp.arange(S), B), PAST + jnp.repeat(jnp.arange(T), NB)])
    cos_rows, sin_rows = _rope_tables(pos_rows, DR)

    h1 = pl.pallas_call(
        functools.partial(_norm_kernel, n_prompt_tiles=np_tiles),
        grid=(n_tiles,),
        in_specs=[xp_spec, xs_spec, full_spec((1, D)), mod_spec(1), mod_spec(0)],
        out_specs=row_spec(D),
        out_shape=jax.ShapeDtypeStruct((R, D), BF16),
        compiler_params=_params("parallel"),
    )(x_p, x_s, P['g_norm_mix'].reshape(1, D), mod_all, mod_all)

    assert T >= 2 and S % RS == 0 and RS >= 2 * NB
    tc = _divisor_tile(C, 512, LANES)
    n_c = C // tc
    tiles_per_seq = S // RS
    n_ct = R // RS
    hist_t = state_conv_l.transpose(1, 0, 2).reshape(2 * NB, C)
    z_all, st_all = pl.pallas_call(
        functools.partial(_conv_kernel, tiles_per_seq=tiles_per_seq, n_prompt_tiles=RP // RS, nb=NB),
        grid=(n_ct, n_c),
        in_specs=[pl.BlockSpec((RS, D), lambda i, j: (i, 0)),
                  pl.BlockSpec((D, tc), lambda i, j: (0, j)),
                  pl.BlockSpec((D, tc), lambda i, j: (0, j)),
                  pl.BlockSpec((D, tc), lambda i, j: (0, j)),
                  pl.BlockSpec((3, tc), lambda i, j: (0, j)),
                  pl.BlockSpec((2 * NB, tc), lambda i, j: (0, j))],
        out_specs=[pl.BlockSpec((RS, tc), lambda i, j: (i, j)),
                   pl.BlockSpec((1, 2 * NB, tc), lambda i, j: (i, 0, j))],
        out_shape=[jax.ShapeDtypeStruct((R, C), BF16), jax.ShapeDtypeStruct((n_ct, 2 * NB, C), F32)],
        scratch_shapes=[pltpu.VMEM((n_c, SUBLANES, tc), F32)],
        compiler_params=_params("arbitrary", "arbitrary"),
    )(h1, w_h, w_b, w_c, P['w_conv'], hist_t)
    new_conv_p = st_all[tiles_per_seq - 1:RP // RS:tiles_per_seq, 2 * NB - 2:]
    new_conv_s = st_all[n_ct - 1].reshape(2, NB, C).transpose(1, 0, 2)
    sblk = RP // RS

    half = DR // 2
    cos_t = cos_rows[:, :half].T
    sin_t = sin_rows[:, half:].T
    tok_spec = pl.BlockSpec((half, tm), lambda i: (0, i))
    qt_all = pl.pallas_call(
        functools.partial(_q_kernel, n_heads=H, d_nope=DN, d_rope=DR, scale=scale, n_prompt_tiles=np_tiles),
        grid=(n_tiles,),
        in_specs=[row_spec(D), full_spec((D, QR)), full_spec((1, QR)), full_spec((H * DQK, QR)),
                  full_spec((DN, 1)), full_spec((DR, 1)), tok_spec, tok_spec],
        out_specs=pl.BlockSpec((H, 1, DQK, tm), lambda i: (0, i, 0, 0)),
        out_shape=jax.ShapeDtypeStruct((H, n_tiles, DQK, tm), BF16),
        compiler_params=_params("parallel"),
    )(h1, w_qd, P['g_q_lat'].reshape(1, QR), w_up.T, gq_n.reshape(DN, 1), gq_p.reshape(DR, 1), cos_t, sin_t)
    q_all = qt_all.transpose(0, 1, 3, 2).reshape(H, R, DQK)

    w_ukt = w_uk.T
    gk_nc, gk_pc = gk_n.reshape(DN, 1), gk_p.reshape(DR, 1)
    ckv_all, kpe_all, kt_all, v_all = pl.pallas_call(
        functools.partial(_kv_kernel, n_heads=H, d_nope=DN, d_rope=DR),
        grid=(n_tiles,),
        in_specs=[row_spec(D), full_spec((D, KVR)), full_spec((D, DR)), full_spec((DR, D)), full_spec((1, KVR)),
                  full_spec((H * DN, KVR)), full_spec((KVR, H * DV)), full_spec((DN, 1)), full_spec((DR, 1)),
                  pl.BlockSpec((half, tm), lambda i: (0, i)), pl.BlockSpec((half, tm), lambda i: (0, i))],
        out_specs=[row_spec(KVR), row_spec(DR),
                   pl.BlockSpec((H, 1, DQK, tm), lambda i: (0, i, 0, 0)), row_spec(H * DV)],
        out_shape=[jax.ShapeDtypeStruct((R, KVR), F32), jax.ShapeDtypeStruct((R, DR), F32),
                   jax.ShapeDtypeStruct((H, n_tiles, DQK, tm), BF16),
                   jax.ShapeDtypeStruct((R, H * DV), BF16)],
        compiler_params=_params("parallel"),
    )(h1, w_kvd, w_kpe, w_kpe.T, P['g_kv_lat'].reshape(1, KVR), w_ukt, w_uv, gk_nc, gk_pc, cos_t, sin_t)

    tq = tm
    nq = S // tq
    o_p = pl.pallas_call(
        functools.partial(_flash_kernel, tq=tq),
        grid=(B, H, nq),
        in_specs=[pl.BlockSpec((1, tq, DQK), lambda b, h, q: (h, b * nq + q, 0)),
                  pl.BlockSpec((1, nq, DQK, tq), lambda b, h, q: (h, b, 0, 0)),
                  pl.BlockSpec((S, DV), lambda b, h, q: (b, h))],
        out_specs=pl.BlockSpec((tq, DV), lambda b, h, q: (b * nq + q, h)),
        out_shape=jax.ShapeDtypeStruct((RP, H * DV), BF16),
        scratch_shapes=[pltpu.VMEM((S, 2 * DV), BF16), pltpu.VMEM((tq, LANES), F32),
                        pltpu.VMEM((tq, 2 * DV), F32)],
        compiler_params=_params("parallel", "parallel", "arbitrary"),
    )(q_all, kt_all, v_all)

    qt = pl.pallas_call(
        _qabs_kernel,
        grid=(H,),
        in_specs=[pl.BlockSpec((1, RS, DN), lambda h: (h, sblk, 0)), full_spec((1, DN)),
                  pl.BlockSpec((KVR, DN), lambda h: (0, h))],
        out_specs=pl.BlockSpec((1, RS, KVR), lambda h: (h, 0, 0)),
        out_shape=jax.ShapeDtypeStruct((H, RS, KVR), BF16),
        compiler_params=_params("parallel"),
    )(q_all, gk_n, w_uk)
    QROWS = T * H
    qt = qt.reshape(H, T, NB, KVR).transpose(2, 1, 0, 3).reshape(NB, QROWS, KVR)
    qpe = q_all[:, RP:, DN:].reshape(H, T, NB, DR).transpose(2, 1, 0, 3).reshape(NB, QROWS, DR)
    pos_keys = jnp.arange(PAST + PAGE)
    cos_k, sin_k = _rope_tables(pos_keys, DR)
    cos_kt, sin_kt = cos_k[:, :half].T, sin_k[:, half:].T
    t_pad = -(-T // SUBLANES) * SUBLANES
    ckv_s = ckv_all[RP:].reshape(T, NB, KVR).transpose(1, 0, 2)
    kpe_s = kpe_all[RP:].reshape(T, NB, DR).transpose(1, 0, 2)
    cn = jnp.pad(ckv_s, ((0, 0), (0, t_pad - T), (0, 0)))
    knt = jnp.pad(kpe_s.transpose(0, 2, 1), ((0, 0), (0, 0), (0, PAGE - T)))
    key_j = jnp.arange(PAGE)[None, :]
    row_t = (jnp.arange(QROWS) // H)[:, None]
    mask_new = ((key_j <= row_t) & (key_j < T)).astype(F32)
    cache_kpe_t = jnp.swapaxes(cache_kpe, 2, 3)
    pps = _divisor_tile(NPAGES, PAGES_PER_STEP, 1)
    n_steps = NPAGES // pps

    def cfull(shape):
        nd = len(shape)
        return pl.BlockSpec(shape, lambda b, p, pt: (0,) * nd)

    o_lat = pl.pallas_call(
        functools.partial(_paged_kernel, pps=pps, layer=l, n_heads=H, d_qk=DQK, t_new=T),
        grid_spec=pltpu.PrefetchScalarGridSpec(
            num_scalar_prefetch=1,
            grid=(NB, n_steps),
            in_specs=([pl.BlockSpec((1, QROWS, KVR), lambda b, p, pt: (b, 0, 0)),
                       pl.BlockSpec((1, QROWS, DR), lambda b, p, pt: (b, 0, 0)),
                       pl.BlockSpec(memory_space=pl.ANY), pl.BlockSpec(memory_space=pl.ANY)]
                      + [pl.BlockSpec((half, pps * PAGE), lambda b, p, pt: (0, p)),
                         pl.BlockSpec((half, pps * PAGE), lambda b, p, pt: (0, p)),
                         pl.BlockSpec((1, t_pad, KVR), lambda b, p, pt: (b, 0, 0)),
                         pl.BlockSpec((1, DR, PAGE), lambda b, p, pt: (b, 0, 0)),
                         pl.BlockSpec((half, PAGE), lambda b, p, pt: (0, NPAGES)),
                         pl.BlockSpec((half, PAGE), lambda b, p, pt: (0, NPAGES)),
                         cfull((QROWS, PAGE)), cfull((H * DN, KVR)), cfull((DR, 1))]),
            out_specs=pl.BlockSpec((1, QROWS, KVR), lambda b, p, pt: (b, 0, 0)),
            scratch_shapes=[pltpu.VMEM((QROWS, 1), F32), pltpu.VMEM((QROWS, 1), F32),
                            pltpu.VMEM((QROWS, KVR), F32), pltpu.VMEM((2, pps * PAGE, KVR), BF16),
                            pltpu.VMEM((2, DR, pps * PAGE), BF16), pltpu.VMEM((2, H, pps * PAGE), F32),
                            pltpu.VMEM((2, pps, PAGE, KVR), F32), pltpu.VMEM((2, pps, DR, PAGE), F32),
                            pltpu.SemaphoreType.DMA((2, 2))]),
        out_shape=jax.ShapeDtypeStruct((NB, QROWS, KVR), BF16),
        compiler_params=_params("arbitrary", "arbitrary"),
    )(page_table.reshape(NB * NPAGES), qt, qpe, cache_ckv, cache_kpe_t, cos_kt, sin_kt,
      cn, knt, cos_kt, sin_kt, mask_new, w_ukt, gk_pc)
    o_lat = o_lat.reshape(NB, T, H, KVR).transpose(2, 1, 0, 3).reshape(H, RS, KVR)
    o_s = pl.pallas_call(
        _ouv_kernel,
        grid=(H,),
        in_specs=[pl.BlockSpec((1, RS, KVR), lambda h: (h, 0, 0)),
                  pl.BlockSpec((KVR, DV), lambda h: (0, h))],
        out_specs=pl.BlockSpec((RS, DV), lambda h: (0, h)),
        out_shape=jax.ShapeDtypeStruct((RS, H * DV), BF16),
        compiler_params=_params("parallel"),
    )(o_lat, w_uv)

    tn = _divisor_tile(D, 512, LANES)
    tmm = _divisor_tile(int(np.gcd(RP, RS)), 1024, SUBLANES)
    npm = RP // tmm
    m_all = pl.pallas_call(
        functools.partial(_mix_kernel, n_prompt_tiles=npm),
        grid=(R // tmm, D // tn),
        in_specs=[pl.BlockSpec((tmm, D), lambda i, j: (i, 0)),
                  pl.BlockSpec((tmm, C), lambda i, j: (i, 0)),
                  pl.BlockSpec((tmm, H * DV), lambda i, j: (jnp.minimum(i, npm - 1), 0)),
                  pl.BlockSpec((tmm, H * DV), lambda i, j: (jnp.maximum(i - npm, 0), 0)),
                  pl.BlockSpec((D, tn), lambda i, j: (0, j)),
                  pl.BlockSpec((D, tn), lambda i, j: (0, j)),
                  pl.BlockSpec((C, tn), lambda i, j: (0, j)),
                  pl.BlockSpec((H * DV, tn), lambda i, j: (0, j))],
        out_specs=pl.BlockSpec((tmm, tn), lambda i, j: (i, j)),
        out_shape=jax.ShapeDtypeStruct((R, D), BF16),
        compiler_params=_params("parallel", "parallel"),
    )(h1, z_all, o_p, o_s, w_ga, w_gb, w_co, w_ao)

    ncol = -(-(G + E) // LANES) * LANES
    w_r = jnp.pad(jnp.concatenate([P['w_route_group'], P['w_route_expert']], axis=1),
                  ((0, 0), (0, ncol - G - E))).astype(BF16)
    b_r = jnp.pad(jnp.concatenate([P['b_route_group'], P['b_route_expert']]), (0, ncol - G - E)).reshape(1, ncol)
    colv = np.arange(ncol)
    gid = np.where((colv >= G) & (colv < G + E), (colv - G) // (E // G), -1).astype(np.float32).reshape(1, ncol)
    x1_all, h2_all, meta_all, ew_all, cnt = pl.pallas_call(
        functools.partial(_wo_kernel, n_groups=G, n_prompt_tiles=np_tiles),
        grid=(n_tiles,),
        in_specs=[row_spec(D), full_spec((D, D)), xp_spec, xs_spec, mod_spec(2), full_spec((1, D)),
                  mod_spec(4), mod_spec(3), full_spec((D, ncol)), full_spec((1, ncol)), full_spec((1, ncol))],
        out_specs=[row_spec(D), row_spec(D), row_spec(ncol), row_spec(ncol), full_spec((SUBLANES, ncol))],
        out_shape=[jax.ShapeDtypeStruct((R, D), F32), jax.ShapeDtypeStruct((R, D), F32),
                   jax.ShapeDtypeStruct((R, ncol), jnp.int32), jax.ShapeDtypeStruct((R, ncol), F32),
                   jax.ShapeDtypeStruct((SUBLANES, ncol), F32)],
        scratch_shapes=[pltpu.VMEM((1, ncol), F32)],
        compiler_params=_params("arbitrary"),
    )(m_all, w_o, x_p, x_s, mod_all, P['g_norm_ffn'].reshape(1, D), mod_all, mod_all, w_r, b_r, jnp.asarray(gid))

    nk = R * TOP_K
    eid = meta_all[:, :TOP_K]
    rank = meta_all[:, TOP_K:2 * TOP_K]
    counts = cnt[0, G:G + E].astype(jnp.int32)
    padded = (counts + MOE_BLOCK - 1) // MOE_BLOCK * MOE_BLOCK
    pad_end = jnp.cumsum(padded)
    pad_start = pad_end - padded
    eid_start = jnp.sum(jnp.where(eid[..., None] == jnp.arange(E, dtype=jnp.int32), pad_start, 0), axis=-1)
    dest = (eid_start + rank).astype(jnp.int32)
    n_blocks = (nk + E * (MOE_BLOCK - 1) + MOE_BLOCK - 1) // MOE_BLOCK
    n_slots = n_blocks * MOE_BLOCK
    flat_t = jnp.repeat(jnp.arange(R, dtype=jnp.int32), TOP_K)
    slot_tok = jnp.zeros((n_slots,), jnp.int32).at[dest.reshape(nk)].set(flat_t, unique_indices=True)
    block_start = jnp.arange(n_blocks, dtype=jnp.int32) * MOE_BLOCK
    block_exp = jnp.minimum(jnp.sum(pad_end[None, :] <= block_start[:, None], axis=1), E - 1).astype(jnp.int32)

    first_blk = jnp.concatenate([jnp.ones((1,), jnp.int32),
                                 (block_exp[1:] != block_exp[:-1]).astype(jnp.int32)])
    group_par = ((jnp.cumsum(first_blk) - 1) % 2).astype(jnp.int32)
    next_start = jnp.sum(block_exp[None, :] <= block_exp[:, None], axis=1)
    next_exp = block_exp[jnp.minimum(next_start, n_blocks - 1)]

    a_all = pl.pallas_call(
        _moe_up_kernel,
        grid_spec=pltpu.PrefetchScalarGridSpec(
            num_scalar_prefetch=5,
            grid=(n_blocks,),
            in_specs=[pl.BlockSpec(memory_space=pl.ANY), pl.BlockSpec(memory_space=pl.ANY),
                      pl.BlockSpec(memory_space=pl.ANY)],
            out_specs=pl.BlockSpec((MOE_BLOCK, F), lambda b, *_: (b, 0)),
            scratch_shapes=[pltpu.VMEM((2, MOE_BLOCK, D), F32), pltpu.VMEM((2, 2, D, F), F32),
                            pltpu.VMEM((D, F), BF16), pltpu.VMEM((D, F), BF16),
                            pltpu.SemaphoreType.DMA((2,)), pltpu.SemaphoreType.DMA((2,))]),
        out_shape=jax.ShapeDtypeStruct((n_slots, F), BF16),
        compiler_params=_params("arbitrary"),
    )(block_exp, first_blk, next_exp, group_par, slot_tok, h2_all, P['w_gate_e'], P['w_up_e'])

    y_slots = pl.pallas_call(
        _moe_down_kernel,
        grid_spec=pltpu.PrefetchScalarGridSpec(
            num_scalar_prefetch=4,
            grid=(n_blocks,),
            in_specs=[pl.BlockSpec((MOE_BLOCK, F), lambda b, *_: (b, 0)),
                      pl.BlockSpec(memory_space=pl.ANY)],
            out_specs=pl.BlockSpec((MOE_BLOCK, D), lambda b, *_: (b, 0)),
            scratch_shapes=[pltpu.VMEM((2, 1, F, D), F32), pltpu.VMEM((F, D), BF16),
                            pltpu.SemaphoreType.DMA((2,))]),
        out_shape=jax.ShapeDtypeStruct((n_slots, D), F32),
        compiler_params=_params("arbitrary"),
    )(block_exp, first_blk, next_exp, group_par, a_all, P['w_down_e'])

    npc = RP // MG
    y_p, y_s = pl.pallas_call(
        functools.partial(_combine_kernel, n_prompt_steps=npc),
        grid_spec=pltpu.PrefetchScalarGridSpec(
            num_scalar_prefetch=2,
            grid=(R // MG,),
            in_specs=[pl.BlockSpec(memory_space=pl.ANY),
                      pl.BlockSpec((MG, D), lambda i, d0, d1: (i, 0)),
                      pl.BlockSpec((1, MG, D),
                                   lambda i, d0, d1: (jnp.where(i * MG < RP, (i * MG) // S, B), 0, 5)),
                      pl.BlockSpec((MG, ncol), lambda i, d0, d1: (i, 0))],
            out_specs=[pl.BlockSpec((MG, D), lambda i, d0, d1: (jnp.minimum(i, npc - 1), 0)),
                       pl.BlockSpec((MG, D), lambda i, d0, d1: (jnp.maximum(i - npc, 0), 0))],
            scratch_shapes=[pltpu.VMEM((2, TOP_K, MG, D), F32), pltpu.SemaphoreType.DMA((2,))]),
        out_shape=[jax.ShapeDtypeStruct((RP, D), F32), jax.ShapeDtypeStruct((RS, D), F32)],
        compiler_params=_params("arbitrary"),
    )(dest[:, 0], dest[:, 1], y_slots, x1_all, mod_all, ew_all)

    return y_p, y_s, new_conv_p, new_conv_s, ckv_all, kpe_all


def kernel(x_prompt, x_sample, state_conv, cache_ckv, cache_kpe, page_table, c_prompt, c_sample, w_ada, b_ada, g_norm_mix, g_norm_ffn, w_in, w_conv, w_conv_out, g_q_lat, w_q_up, g_kv_lat, w_uk, w_uv, g_qk_q, g_qk_k, w_attn_out, w_o, w_route_group, b_route_group, w_route_expert, b_route_expert, w_gate_e, w_up_e, w_down_e):
    B, S, D = x_prompt.shape
    NB, T, _ = x_sample.shape
    depth = w_in.shape[0]
    RP = B * S
    KVR = cache_ckv.shape[-1]
    DR = cache_kpe.shape[-1]
    C = w_conv.shape[-1]
    x_p = x_prompt.reshape(RP, D)
    x_s = x_sample.transpose(1, 0, 2).reshape(T * NB, D)
    c_rows = jnp.concatenate([jnp.repeat(c_prompt, NB, axis=0), c_sample], axis=0)
    weights = dict(w_ada=w_ada, b_ada=b_ada, g_norm_mix=g_norm_mix, g_norm_ffn=g_norm_ffn, w_in=w_in,
                   w_conv=w_conv, w_conv_out=w_conv_out, g_q_lat=g_q_lat, w_q_up=w_q_up, g_kv_lat=g_kv_lat,
                   w_uk=w_uk, w_uv=w_uv, g_qk_q=g_qk_q, g_qk_k=g_qk_k, w_attn_out=w_attn_out, w_o=w_o,
                   w_route_group=w_route_group, b_route_group=b_route_group, w_route_expert=w_route_expert,
                   b_route_expert=b_route_expert, w_gate_e=w_gate_e, w_up_e=w_up_e, w_down_e=w_down_e)
    conv_p, conv_s, ckv_p, kpe_p, ckv_s, kpe_s = [], [], [], [], [], []
    for l in range(depth):
        P = {k: v[l] for k, v in weights.items()}
        mod_all = _adaln(c_rows, P['w_ada'], P['b_ada']).reshape(B + 1, NB, 6 * D)
        x_p, x_s, cp, cs, ckv_all, kpe_all = _layer(x_p, x_s, (B, S, NB, T), mod_all, state_conv[l], cache_ckv,
                                                    cache_kpe, page_table, l, P)
        conv_p.append(cp)
        conv_s.append(cs)
        ckv_p.append(ckv_all[:RP].reshape(B, S, KVR))
        kpe_p.append(kpe_all[:RP].reshape(B, S, DR))
        ckv_s.append(ckv_all[RP:].reshape(T, NB, KVR).transpose(1, 0, 2))
        kpe_s.append(kpe_all[RP:].reshape(T, NB, DR).transpose(1, 0, 2))
    y_prompt = x_p.reshape(B, S, D)
    y_sample = x_s.reshape(T, NB, D).transpose(1, 0, 2)
    return (y_prompt, y_sample, jnp.stack(conv_p), jnp.stack(conv_s), jnp.stack(ckv_p), jnp.stack(kpe_p),
            jnp.stack(ckv_s), jnp.stack(kpe_s))
```
